```python
import math
import jax, jax.numpy as jnp
from jax import lax
import numpy as np

D_MODEL = 1024
BATCH = 8
SEQ = 2048
DEPTH = 1
DEC_BATCH = 128
DEC_SEQ = 4
PAST_LEN = 2048
PAGE_SIZE = 128

ATT_HEADS = 8
ATT_HEAD_DIM = 64
ATT_QK_W = ATT_HEADS * 2 * ATT_HEAD_DIM
ATT_V_W = ATT_HEADS * 2 * ATT_HEAD_DIM
Q_BLOCK = 128
S5_CH = 16
S5_WIDTH = D_MODEL
S5_GROUPS = S5_WIDTH // S5_CH
S5_STATE = 64
IN_COLS = 2 * ATT_QK_W + ATT_V_W + S5_WIDTH + 2 * D_MODEL
IN_SPLITS = (ATT_QK_W,
             2 * ATT_QK_W,
             2 * ATT_QK_W + ATT_V_W,
             2 * ATT_QK_W + ATT_V_W + S5_WIDTH,
             2 * ATT_QK_W + ATT_V_W + S5_WIDTH + D_MODEL)
PEER_HEADS = 8
PEER_KEY_DIM = 256
PEER_N_KEYS = 128
PEER_N_EXPERTS = PEER_N_KEYS * PEER_N_KEYS
PEER_TOPK = 16
PEER_BLOCK = 128
RMS_EPS = 1e-6

kernel_name = "hybrid_diffattn_s5_peer_step"


def rms_norm(x, g):
    xf = x.astype(jnp.float32)
    y = xf * lax.rsqrt(jnp.mean(xf * xf, axis=-1, keepdims=True) + RMS_EPS)
    return (y * g.astype(jnp.float32)).astype(x.dtype)


def lambda_init(layer):
    return 0.8 - 0.6 * math.exp(-0.3 * layer)


def diff_attention(q, k, v, q_pos, k_pos, lam):
    q1, q2 = q[..., :ATT_HEAD_DIM], q[..., ATT_HEAD_DIM:]
    k1, k2 = k[..., :ATT_HEAD_DIM], k[..., ATT_HEAD_DIM:]
    scale = ATT_HEAD_DIM ** -0.5
    mask = k_pos[None, :] <= q_pos[:, None]

    def probs(qa, ka):
        s = jnp.einsum('bqhd,bkhd->bhqk', qa, ka).astype(jnp.float32) * scale
        s = jnp.where(mask, s, -jnp.inf)
        return jax.nn.softmax(s, axis=-1)

    p = probs(q1, k1) - lam * probs(q2, k2)
    return jnp.einsum('bhqk,bkhe->bqhe', p.astype(v.dtype), v)


def prompt_attention(q, k, v, lam):
    B, S = q.shape[:2]
    nb = S // Q_BLOCK
    qb = jnp.moveaxis(q.reshape(B, nb, Q_BLOCK, ATT_HEADS, 2 * ATT_HEAD_DIM), 1, 0)
    k_pos = jnp.arange(S)

    def one(args):
        i, qi = args
        q_pos = i * Q_BLOCK + jnp.arange(Q_BLOCK)
        return diff_attention(qi, k, v, q_pos, k_pos, lam)

    ob = lax.map(one, (jnp.arange(nb), qb))
    return jnp.moveaxis(ob, 0, 1).reshape(B, S, ATT_HEADS, 2 * ATT_HEAD_DIM)


def sample_attention(q, k_new, v_new, k_past, v_past, lam):
    past = k_past.shape[1]
    n_new = q.shape[1]
    k = jnp.concatenate([k_past, k_new], axis=1)
    v = jnp.concatenate([v_past, v_new], axis=1)
    q_pos = past + jnp.arange(n_new)
    k_pos = jnp.arange(past + n_new)
    return diff_attention(q, k, v, q_pos, k_pos, lam)


def s5_scan(u, h0_re, h0_im, a_re, a_im, log_step, b_re, b_im, c_re, c_im, d_skip):
    f32 = jnp.float32
    Bsz, L, _ = u.shape
    ug = u.reshape(Bsz, L, S5_GROUPS, S5_CH).astype(f32)
    a_re, a_im = a_re.astype(f32), a_im.astype(f32)
    dt = jnp.exp(log_step.astype(f32))[:, None]
    mag = jnp.exp(a_re * dt)
    ang = a_im * dt
    lb_re, lb_im = mag * jnp.cos(ang), mag * jnp.sin(ang)
    num_re, num_im = lb_re - 1.0, lb_im
    den = a_re * a_re + a_im * a_im
    f_re = (num_re * a_re + num_im * a_im) / den
    f_im = (num_im * a_re - num_re * a_im) / den
    b_re, b_im = b_re.astype(f32), b_im.astype(f32)
    bb_re = f_re[..., None] * b_re - f_im[..., None] * b_im
    bb_im = f_re[..., None] * b_im + f_im[..., None] * b_re
    bu_re = jnp.einsum('blgc,gpc->blgp', ug, bb_re)
    bu_im = jnp.einsum('blgc,gpc->blgp', ug, bb_im)
    h0_re, h0_im = h0_re.astype(f32), h0_im.astype(f32)
    bu_re = bu_re.at[:, 0].add(lb_re * h0_re - lb_im * h0_im)
    bu_im = bu_im.at[:, 0].add(lb_re * h0_im + lb_im * h0_re)
    ar = jnp.broadcast_to(lb_re, (1, L, S5_GROUPS, S5_STATE))
    ai = jnp.broadcast_to(lb_im, (1, L, S5_GROUPS, S5_STATE))

    def combine(e1, e2):
        a1r, a1i, b1r, b1i = e1
        a2r, a2i, b2r, b2i = e2
        return (a2r * a1r - a2i * a1i,
                a2r * a1i + a2i * a1r,
                a2r * b1r - a2i * b1i + b2r,
                a2r * b1i + a2i * b1r + b2i)

    _, _, hr, hi = lax.associative_scan(combine, (ar, ai, bu_re, bu_im), axis=1)
    y = (jnp.einsum('blgp,gcp->blgc', hr, c_re.astype(f32))
         - jnp.einsum('blgp,gcp->blgc', hi, c_im.astype(f32))
         + d_skip.astype(f32) * ug)
    return y.reshape(Bsz, L, S5_WIDTH).astype(u.dtype), hr[:, -1], hi[:, -1]


def peer_ffn(h, w_query, sub_keys, expert_u, expert_v):
    B, L, D = h.shape
    T = B * L
    n_blk = -(-T // PEER_BLOCK)
    pad = n_blk * PEER_BLOCK - T
    hf = jnp.pad(h.reshape(T, D), ((0, pad), (0, 0))).reshape(n_blk, PEER_BLOCK, D)
    half = PEER_KEY_DIM // 2

    def one(xb):
        q = (xb @ w_query).reshape(PEER_BLOCK, PEER_HEADS, PEER_KEY_DIM)
        s1 = jnp.einsum('thd,hkd->thk', q[..., :half], sub_keys[:, 0]).astype(jnp.float32)
        s2 = jnp.einsum('thd,hkd->thk', q[..., half:], sub_keys[:, 1]).astype(jnp.float32)
        v1, i1 = lax.top_k(s1, PEER_TOPK)
        v2, i2 = lax.top_k(s2, PEER_TOPK)
        cand = (v1[..., :, None] + v2[..., None, :]).reshape(PEER_BLOCK, PEER_HEADS, PEER_TOPK * PEER_TOPK)
        cidx = (i1[..., :, None] * PEER_N_KEYS + i2[..., None, :]).reshape(PEER_BLOCK, PEER_HEADS, PEER_TOPK * PEER_TOPK)
        sv, si = lax.top_k(cand, PEER_TOPK)
        eidx = jnp.take_along_axis(cidx, si, axis=-1)
        gate = jax.nn.softmax(sv, axis=-1)
        u_rows = expert_u[eidx]
        act = jax.nn.gelu(jnp.einsum('thkd,td->thk', u_rows, xb), approximate=False)
        w = (gate * act.astype(jnp.float32)).astype(xb.dtype)
        return jnp.einsum('thk,thkd->td', w, expert_v[eidx])

    out = lax.map(one, hf).reshape(n_blk * PEER_BLOCK, D)[:T]
    return out.reshape(B, L, D)


def trunk_layer(x, attend, h0_re, h0_im, lw, lam_init):
    B, L = x.shape[:2]
    h = rms_norm(x, lw['norm_mix_g'])
    proj = h @ lw['w_in']
    q, k, v, u, g_att, g_s5 = jnp.split(proj, IN_SPLITS, axis=-1)
    hs = (B, L, ATT_HEADS, 2 * ATT_HEAD_DIM)
    q, k, v = q.reshape(hs), k.reshape(hs), v.reshape(hs)
    att = attend(q, k, v)
    att = (rms_norm(att, lw['subln_g']) * (1.0 - lam_init)).reshape(B, L, ATT_V_W)
    s5y, hT_re, hT_im = s5_scan(u, h0_re, h0_im, lw['s5_a_re'], lw['s5_a_im'], lw['s5_log_step'],
                                lw['s5_b_re'], lw['s5_b_im'], lw['s5_c_re'], lw['s5_c_im'], lw['s5_d'])
    z = jax.nn.gelu(s5y, approximate=False)
    s5o = z * jax.nn.sigmoid(z @ lw['w_glu'] + lw['b_glu'])
    merged = (jax.nn.sigmoid(g_att) * (att @ lw['w_proj_att'])
              + jax.nn.sigmoid(g_s5) * (s5o @ lw['w_proj_s5']))
    x = x + merged @ lw['w_out']
    x = x + peer_ffn(rms_norm(x, lw['norm_ffn_g']), lw['peer_w_query'], lw['peer_sub_keys'],
                     lw['peer_u'], lw['peer_v'])
    return x, k, v, hT_re, hT_im


def setup_inputs(seed: int = 0) -> dict:
    key = jax.random.key(seed)
    ks = iter(jax.random.split(key, 40))
    f32 = jnp.float32

    def nrm(shape, scale):
        return jax.random.normal(next(ks), shape, f32) * scale

    HD2 = 2 * ATT_HEAD_DIM
    n_pages = PAST_LEN // PAGE_SIZE
    n_used = DEC_BATCH * n_pages
    n_phys = n_used + (n_used + 3) // 4
    perm = jax.random.permutation(next(ks), n_phys)
    page_table = perm[:n_used].reshape(DEC_BATCH, n_pages).astype(jnp.int32)
    G, P, C = S5_GROUPS, S5_STATE, S5_CH
    return {
        'x_prompt': nrm((BATCH, SEQ, D_MODEL), 1.0),
        'x_sample': nrm((DEC_BATCH, DEC_SEQ, D_MODEL), 1.0),
        'cache_k': nrm((DEPTH, n_phys, PAGE_SIZE, ATT_HEADS, HD2), 1.0),
        'cache_v': nrm((DEPTH, n_phys, PAGE_SIZE, ATT_HEADS, HD2), 1.0),
        'state_s5_re': nrm((DEPTH, DEC_BATCH, G, P), 0.3),
        'state_s5_im': nrm((DEPTH, DEC_BATCH, G, P), 0.3),
        'page_table': page_table,
        'norm_mix_g': 1.0 + nrm((DEPTH, D_MODEL), 0.02),
        'w_in': nrm((DEPTH, D_MODEL, IN_COLS), D_MODEL ** -0.5),
        'lambda_q1': nrm((DEPTH, ATT_HEAD_DIM), 0.1),
        'lambda_k1': nrm((DEPTH, ATT_HEAD_DIM), 0.1),
        'lambda_q2': nrm((DEPTH, ATT_HEAD_DIM), 0.1),
        'lambda_k2': nrm((DEPTH, ATT_HEAD_DIM), 0.1),
        'subln_g': 1.0 + nrm((DEPTH, HD2), 0.02),
        's5_a_re': -0.5 + nrm((DEPTH, G, P), 0.01),
        's5_a_im': jnp.pi * jnp.arange(P, dtype=f32) + nrm((DEPTH, G, P), 0.01),
        's5_log_step': jax.random.uniform(next(ks), (DEPTH, G), f32, math.log(1e-3), math.log(1e-1)),
        's5_b_re': nrm((DEPTH, G, P, C), C ** -0.5),
        's5_b_im': nrm((DEPTH, G, P, C), C ** -0.5),
        's5_c_re': nrm((DEPTH, G, C, P), P ** -0.5),
        's5_c_im': nrm((DEPTH, G, C, P), P ** -0.5),
        's5_d': nrm((DEPTH, G, C), 1.0),
        'w_glu': nrm((DEPTH, S5_WIDTH, S5_WIDTH), S5_WIDTH ** -0.5),
        'b_glu': nrm((DEPTH, S5_WIDTH), 0.02),
        'w_proj_att': nrm((DEPTH, ATT_V_W, D_MODEL), ATT_V_W ** -0.5),
        'w_proj_s5': nrm((DEPTH, S5_WIDTH, D_MODEL), S5_WIDTH ** -0.5),
        'w_out': nrm((DEPTH, D_MODEL, D_MODEL), D_MODEL ** -0.5),
        'norm_ffn_g': 1.0 + nrm((DEPTH, D_MODEL), 0.02),
        'peer_w_query': nrm((DEPTH, D_MODEL, PEER_HEADS * PEER_KEY_DIM), D_MODEL ** -0.5),
        'peer_sub_keys': nrm((DEPTH, PEER_HEADS, 2, PEER_N_KEYS, PEER_KEY_DIM // 2), (PEER_KEY_DIM // 2) ** -0.5),
        'peer_u': nrm((DEPTH, PEER_N_EXPERTS, D_MODEL), D_MODEL ** -0.5),
        'peer_v': nrm((DEPTH, PEER_N_EXPERTS, D_MODEL), PEER_HEADS ** -0.5),
        'norm_final_g': 1.0 + nrm((D_MODEL,), 0.02),
    }


def reference(x_prompt, x_sample, cache_k, cache_v, state_s5_re, state_s5_im, page_table,
              norm_mix_g, w_in, lambda_q1, lambda_k1, lambda_q2, lambda_k2, subln_g,
              s5_a_re, s5_a_im, s5_log_step, s5_b_re, s5_b_im, s5_c_re, s5_c_im, s5_d,
              w_glu, b_glu, w_proj_att, w_proj_s5, w_out,
              norm_ffn_g, peer_w_query, peer_sub_keys, peer_u, peer_v, norm_final_g):
    f32 = jnp.float32
    xp, xs = x_prompt, x_sample
    n_seq, n_pages = page_table.shape
    kp, vp, srp, sip, ksm, vsm, srs, sis = [], [], [], [], [], [], [], []
    for l in range(DEPTH):
        lam_init = lambda_init(l)
        lam = (jnp.exp(jnp.sum(lambda_q1[l].astype(f32) * lambda_k1[l].astype(f32)))
               - jnp.exp(jnp.sum(lambda_q2[l].astype(f32) * lambda_k2[l].astype(f32)))
               + lam_init)
        lw = {
            'norm_mix_g': norm_mix_g[l], 'w_in': w_in[l], 'subln_g': subln_g[l],
            's5_a_re': s5_a_re[l], 's5_a_im': s5_a_im[l], 's5_log_step': s5_log_step[l],
            's5_b_re': s5_b_re[l], 's5_b_im': s5_b_im[l], 's5_c_re': s5_c_re[l], 's5_c_im': s5_c_im[l],
            's5_d': s5_d[l], 'w_glu': w_glu[l], 'b_glu': b_glu[l],
            'w_proj_att': w_proj_att[l], 'w_proj_s5': w_proj_s5[l], 'w_out': w_out[l],
            'norm_ffn_g': norm_ffn_g[l], 'peer_w_query': peer_w_query[l],
            'peer_sub_keys': peer_sub_keys[l], 'peer_u': peer_u[l], 'peer_v': peer_v[l],
        }
        h0 = jnp.zeros((xp.shape[0], S5_GROUPS, S5_STATE), f32)
        xp, k_p, v_p, hr_p, hi_p = trunk_layer(
            xp, lambda q, k, v: prompt_attention(q, k, v, lam), h0, h0, lw, lam_init)
        k_past = cache_k[l][page_table].reshape(n_seq, n_pages * PAGE_SIZE, ATT_HEADS, 2 * ATT_HEAD_DIM)
        v_past = cache_v[l][page_table].reshape(n_seq, n_pages * PAGE_SIZE, ATT_HEADS, 2 * ATT_HEAD_DIM)
        xs, k_s, v_s, hr_s, hi_s = trunk_layer(
            xs, lambda q, k, v: sample_attention(q, k, v, k_past, v_past, lam),
            state_s5_re[l], state_s5_im[l], lw, lam_init)
        kp.append(k_p); vp.append(v_p); srp.append(hr_p); sip.append(hi_p)
        ksm.append(k_s); vsm.append(v_s); srs.append(hr_s); sis.append(hi_s)
    y_prompt = rms_norm(xp, norm_final_g)
    y_sample = rms_norm(xs, norm_final_g)
    return (y_prompt, y_sample,
            jnp.stack(kp), jnp.stack(vp), jnp.stack(srp), jnp.stack(sip),
            jnp.stack(ksm), jnp.stack(vsm), jnp.stack(srs), jnp.stack(sis))
```

```python
import functools
import math

import jax
import jax.numpy as jnp
from jax import lax
from jax.experimental import pallas as pl
from jax.experimental.pallas import tpu as pltpu

F32 = jnp.float32
BF16 = jnp.bfloat16

LANES = 128
SUBLANES = 8
VMEM_LIMIT = 56 * 1024 * 1024

RMS_EPS = 1e-6
N_HEADS = 8
HEAD_DIM = 64
HEAD_W = 2 * HEAD_DIM
S5_GROUPS = 64
S5_CH = 16
S5_STATE = 64
S5_GB = 8
N_GB = S5_GROUPS // S5_GB
PEER_HEADS = 8
PEER_KEYS = 128
PEER_HALF = 128
PEER_TOPK = 16
NEG_INF = float("-inf")
NO_RANK = 99.0

_NT = (((1,), (1,)), ((), ()))


def _params(*sem):
    return pltpu.CompilerParams(dimension_semantics=sem, vmem_limit_bytes=VMEM_LIMIT)


def _rms(x, g):
    return x * lax.rsqrt(jnp.mean(x * x, axis=-1, keepdims=True) + RMS_EPS) * g


def _gelu(x):
    return 0.5 * x * (1.0 + lax.erf(x * math.sqrt(0.5)))


def _lam(lq1, lk1, lq2, lk2, lam_init):
    a = jnp.sum(lq1[...] * lk1[...], axis=-1, keepdims=True)
    b = jnp.sum(lq2[...] * lk2[...], axis=-1, keepdims=True)
    return jnp.exp(a) - jnp.exp(b) + lam_init


def _inproj_kernel(x_ref, g_ref, w_ref, q_ref, k_ref, v_ref, u_ref, ga_ref, gs_ref, *, d, q_scale):
    h = _rms(x_ref[...], g_ref[...]).astype(BF16)
    outs = (q_ref, k_ref, v_ref, u_ref, ga_ref, gs_ref)
    for j, o in enumerate(outs):
        r = jnp.dot(h, w_ref[:, j * d:(j + 1) * d], preferred_element_type=F32)
        if j == 0:
            r = r * q_scale
        o[...] = r.astype(o.dtype)


def _inproj(x, g, w, tm):
    t, d = x.shape
    row = lambda i: (i, 0)
    fix = lambda i: (0, 0)
    spec = pl.BlockSpec((tm, d), row)
    return pl.pallas_call(
        functools.partial(_inproj_kernel, d=d, q_scale=HEAD_DIM ** -0.5),
        grid=(t // tm,),
        in_specs=[spec, pl.BlockSpec((1, d), fix), pl.BlockSpec(w.shape, fix)],
        out_specs=[spec] * 6,
        out_shape=[jax.ShapeDtypeStruct((t, d), BF16)] + [jax.ShapeDtypeStruct((t, d), F32)] * 5,
        compiler_params=_params("parallel"),
        name="inproj",
    )(x, g, w)


def _pattn_kernel(q_ref, k_ref, v_ref, lq1, lk1, lq2, lk2, sg_ref, o_ref, kb, vb, *, tq, lam_init):
    qi = pl.program_id(2)

    @pl.when(qi == 0)
    def _():
        kb[...] = k_ref[...].astype(BF16)
        vb[...] = v_ref[...].astype(BF16)

    lam = _lam(lq1, lk1, lq2, lk2, lam_init)
    q = q_ref[...]
    lane = lax.broadcasted_iota(jnp.int32, q.shape, 1)
    zero = jnp.zeros_like(q)
    q1 = jnp.where(lane < HEAD_DIM, q, zero)
    q2 = jnp.where(lane >= HEAD_DIM, q, zero)

    def block(j, carry, diag):
        m1, l1, m2, l2, a1, a2 = carry
        kj = kb[pl.ds(j * tq, tq), :]
        vj = vb[pl.ds(j * tq, tq), :]
        s1 = lax.dot_general(q1, kj, _NT, preferred_element_type=F32)
        s2 = lax.dot_general(q2, kj, _NT, preferred_element_type=F32)
        if diag:
            r = lax.broadcasted_iota(jnp.int32, s1.shape, 0)
            c = lax.broadcasted_iota(jnp.int32, s1.shape, 1)
            s1 = jnp.where(c <= r, s1, NEG_INF)
            s2 = jnp.where(c <= r, s2, NEG_INF)
        n1 = jnp.maximum(m1, jnp.max(s1, axis=-1, keepdims=True))
        n2 = jnp.maximum(m2, jnp.max(s2, axis=-1, keepdims=True))
        p1 = jnp.exp(s1 - n1)
        p2 = jnp.exp(s2 - n2)
        c1 = jnp.exp(m1 - n1)
        c2 = jnp.exp(m2 - n2)
        l1 = c1 * l1 + jnp.sum(p1, axis=-1, keepdims=True)
        l2 = c2 * l2 + jnp.sum(p2, axis=-1, keepdims=True)
        pv = jnp.dot(jnp.concatenate([p1, p2], axis=0).astype(BF16), vj, preferred_element_type=F32)
        a1 = c1 * a1 + pv[:tq]
        a2 = c2 * a2 + pv[tq:]
        return n1, l1, n2, l2, a1, a2

    col = jnp.full((tq, 1), NEG_INF, F32)
    init = (col, jnp.zeros((tq, 1), F32), col, jnp.zeros((tq, 1), F32),
            jnp.zeros((tq, HEAD_W), F32), jnp.zeros((tq, HEAD_W), F32))
    carry = lax.fori_loop(0, qi, lambda j, c: block(j, c, False), init)
    m1, l1, m2, l2, a1, a2 = block(qi, carry, True)
    att = a1 / l1 - lam * (a2 / l2)
    o_ref[...] = (_rms(att, sg_ref[...]) * (1.0 - lam_init)).astype(o_ref.dtype)


def _pattn(q, k, v, lams, sg, nb, s, tq, lam_init):
    t, d = k.shape
    nq = s // tq
    vec = pl.BlockSpec((1, HEAD_DIM), lambda b, h, i: (0, 0))
    qspec = pl.BlockSpec((tq, HEAD_W), lambda b, h, i: (b * nq + i, h))
    kvspec = pl.BlockSpec((s, HEAD_W), lambda b, h, i: (b, h))
    return pl.pallas_call(
        functools.partial(_pattn_kernel, tq=tq, lam_init=lam_init),
        grid=(nb, N_HEADS, nq),
        in_specs=[qspec, kvspec, kvspec, vec, vec, vec, vec,
                  pl.BlockSpec((1, HEAD_W), lambda b, h, i: (0, 0))],
        out_specs=qspec,
        out_shape=jax.ShapeDtypeStruct((t, d), BF16),
        scratch_shapes=[pltpu.VMEM((s, HEAD_W), BF16), pltpu.VMEM((s, HEAD_W), BF16)],
        compiler_params=_params("parallel", "parallel", "arbitrary"),
        name="pattn",
    )(q, k, v, *lams, sg)


def _sattn_kernel(pt_ref, q_ref, kn_ref, vn_ref, ck_ref, cv_ref, lq1, lk1, lq2, lk2, sg_ref, o_ref,
                  kbuf, vbuf, sem, *, n_pages, page, n_new, lam_init):
    s = pl.program_id(0)
    ns = pl.num_programs(0)
    past = n_pages * page
    d = q_ref.shape[-1]
    tail = kbuf.shape[1] - past

    def page_copies(seq, slot):
        out = []
        for j in range(n_pages):
            p = pt_ref[seq, j]
            out.append(pltpu.make_async_copy(ck_ref.at[p], kbuf.at[slot, pl.ds(j * page, page)], sem.at[slot, 0]))
            out.append(pltpu.make_async_copy(cv_ref.at[p], vbuf.at[slot, pl.ds(j * page, page)], sem.at[slot, 1]))
        return out

    @pl.when(s == 0)
    def _():
        for c in page_copies(0, 0):
            c.start()
        z = jnp.zeros((2, tail, d), F32)
        kbuf[:, pl.ds(past, tail), :] = z
        vbuf[:, pl.ds(past, tail), :] = z

    slot = s % 2

    @pl.when(s + 1 < ns)
    def _():
        for c in page_copies(s + 1, 1 - slot):
            c.start()

    kbuf[slot, pl.ds(past, n_new), :] = kn_ref[...]
    vbuf[slot, pl.ds(past, n_new), :] = vn_ref[...]
    for c in page_copies(s, slot):
        c.wait()

    lam = _lam(lq1, lk1, lq2, lk2, lam_init)
    n_rows = 2 * n_new * N_HEADS
    half = n_new * N_HEADS
    qf = q_ref[...].astype(F32)
    qrep = jnp.broadcast_to(qf[None, :, None, :], (2, n_new, N_HEADS, d)).reshape(n_rows, d)
    r = lax.broadcasted_iota(jnp.int32, (n_rows, d), 0)
    c = lax.broadcasted_iota(jnp.int32, (n_rows, d), 1)
    own = (c // HEAD_DIM) == 2 * (r % N_HEADS) + r // half
    qbd = jnp.where(own, qrep, 0.0).astype(BF16)

    n_keys = kbuf.shape[1]
    ck = 256
    pieces = []
    for c0 in range(0, n_keys, ck):
        w = min(ck, n_keys - c0)
        kc = kbuf[slot, pl.ds(c0, w), :].astype(BF16)
        pieces.append(lax.dot_general(qbd, kc, _NT, preferred_element_type=F32))
    sc = jnp.concatenate(pieces, axis=1)
    rr = lax.broadcasted_iota(jnp.int32, sc.shape, 0)
    cc = lax.broadcasted_iota(jnp.int32, sc.shape, 1)
    tok = (rr % half) // N_HEADS
    sc = jnp.where(cc <= past + tok, sc, NEG_INF)
    mx = jnp.max(sc, axis=-1, keepdims=True)
    pr = jnp.exp(sc - mx)
    pr = pr / jnp.sum(pr, axis=-1, keepdims=True)
    pd = (pr[:half] - lam * pr[half:]).astype(BF16)
    acc = jnp.zeros((half, d), F32)
    for c0 in range(0, n_keys, ck):
        w = min(ck, n_keys - c0)
        vc = vbuf[slot, pl.ds(c0, w), :].astype(BF16)
        acc = acc + jnp.dot(pd[:, c0:c0 + w], vc, preferred_element_type=F32)
    r2 = lax.broadcasted_iota(jnp.int32, (half, d), 0)
    c2 = lax.broadcasted_iota(jnp.int32, (half, d), 1)
    mine = (c2 // HEAD_W) == (r2 % N_HEADS)
    acc = jnp.where(mine, acc, 0.0)
    ms = jnp.sum(acc * acc, axis=-1, keepdims=True) * (1.0 / HEAD_W)
    nrm = acc * lax.rsqrt(ms + RMS_EPS) * sg_ref[...] * (1.0 - lam_init)
    o_ref[...] = jnp.sum(nrm.reshape(n_new, N_HEADS, d), axis=1).astype(o_ref.dtype)


def _sattn(page_table, q, kn, vn, cache_k, cache_v, lams, sg_tiled, lam_init):
    ns, n_new, d = q.shape
    n_pages = page_table.shape[1]
    page = cache_k.shape[1]
    n_keys = n_pages * page + LANES
    vec = pl.BlockSpec((1, HEAD_DIM), lambda s, pt: (0, 0))
    tokspec = pl.BlockSpec((None, n_new, d), lambda s, pt: (s, 0, 0))
    anyspec = pl.BlockSpec(memory_space=pl.ANY)
    grid_spec = pltpu.PrefetchScalarGridSpec(
        num_scalar_prefetch=1,
        grid=(ns,),
        in_specs=[tokspec, tokspec, tokspec, anyspec, anyspec, vec, vec, vec, vec,
                  pl.BlockSpec((1, d), lambda s, pt: (0, 0))],
        out_specs=tokspec,
        scratch_shapes=[pltpu.VMEM((2, n_keys, d), F32), pltpu.VMEM((2, n_keys, d), F32),
                        pltpu.SemaphoreType.DMA((2, 2))],
    )
    return pl.pallas_call(
        functools.partial(_sattn_kernel, n_pages=n_pages, page=page, n_new=n_new, lam_init=lam_init),
        grid_spec=grid_spec,
        out_shape=jax.ShapeDtypeStruct((ns, n_new, d), BF16),
        compiler_params=_params("arbitrary"),
        name="sattn",
    )(page_table, q, kn, vn, cache_k, cache_v, *lams, sg_tiled)


def _s5prep_kernel(are_ref, aim_ref, ls_ref, bre_ref, bim_ref, lbr_ref, lbi_ref, bbr_ref, bbi_ref):
    a_re, a_im = are_ref[...], aim_ref[...]
    dt = jnp.exp(ls_ref[...])
    mag = jnp.exp(a_re * dt)
    ang = a_im * dt
    lb_re, lb_im = mag * jnp.cos(ang), mag * jnp.sin(ang)
    num_re, num_im = lb_re - 1.0, lb_im
    den = a_re * a_re + a_im * a_im
    f_re = (num_re * a_re + num_im * a_im) / den
    f_im = (num_im * a_re - num_re * a_im) / den
    b_re, b_im = bre_ref[...], bim_ref[...]
    lbr_ref[...] = lb_re
    lbi_ref[...] = lb_im
    bbr_ref[...] = f_re * b_re - f_im * b_im
    bbi_ref[...] = f_re * b_im + f_im * b_re


def _s5prep(a_re, a_im, log_step, b_re, b_im):
    n = a_re.shape[1]
    c = b_re.shape[0]
    row = jax.ShapeDtypeStruct((1, n), F32)
    mat = jax.ShapeDtypeStruct((c, n), F32)
    return pl.pallas_call(_s5prep_kernel, out_shape=[row, row, mat, mat], name="s5prep")(
        a_re, a_im, log_step, b_re, b_im)


def _s5_kernel(u_ref, h0r_ref, h0i_ref, lbr_ref, lbi_ref, bb_ref, cc_ref, dsk_ref, wg_ref, bg_ref,
               o_ref, hr_ref, hi_ref, st_re, st_im, bu, hb, y, *, tc, nb):
    step = pl.program_id(0)
    gw = S5_GB * S5_STATE

    @pl.when(step == 0)
    def _():
        st_re[...] = h0r_ref[...]
        st_im[...] = h0i_ref[...]

    u = u_ref[...]
    ub = u.astype(BF16)
    for gb in range(N_GB):
        bu[...] = jnp.dot(ub[:, gb * LANES:(gb + 1) * LANES], bb_ref[gb], preferred_element_type=F32)
        cols = pl.ds(gb * gw, gw)
        lr = jnp.broadcast_to(lbr_ref[:, cols], (SUBLANES, gw))
        li = jnp.broadcast_to(lbi_ref[:, cols], (SUBLANES, gw))
        for bt in range(nb // SUBLANES):
            rows = pl.ds(bt * SUBLANES, SUBLANES)

            def tstep(t, carry):
                hr, hi = carry
                r0 = pl.multiple_of(t * nb + bt * SUBLANES, SUBLANES)
                bur = bu[pl.ds(r0, SUBLANES), 0:gw]
                bui = bu[pl.ds(r0, SUBLANES), gw:2 * gw]
                nr = lr * hr - li * hi + bur
                ni = lr * hi + li * hr + bui
                hb[pl.ds(r0, SUBLANES), 0:gw] = nr
                hb[pl.ds(r0, SUBLANES), gw:2 * gw] = ni
                return nr, ni

            hr, hi = lax.fori_loop(0, tc, tstep, (st_re[rows, cols], st_im[rows, cols]))
            st_re[rows, cols] = hr
            st_im[rows, cols] = hi
        y[:, gb * LANES:(gb + 1) * LANES] = jnp.dot(hb[...].astype(BF16), cc_ref[gb],
                                                      preferred_element_type=F32)
    yy = y[...] + dsk_ref[...] * u
    z = _gelu(yy)
    gate = jnp.dot(z.astype(BF16), wg_ref[...], preferred_element_type=F32) + bg_ref[...]
    o_ref[...] = (z * jax.nn.sigmoid(gate)).astype(o_ref.dtype)

    @pl.when(step == pl.num_programs(0) - 1)
    def _():
        hr_ref[...] = st_re[...]
        hi_ref[...] = st_im[...]


def _s5(u_tm, h0_re, h0_im, lb_re, lb_im, bblk, cblk, dskip, w_glu, b_glu, nb, tc):
    rows, d = u_tm.shape
    n_state = h0_re.shape[1]
    m = tc * nb
    fix2 = lambda i: (0, 0)
    fix3 = lambda i: (0, 0, 0)
    blk = pl.BlockSpec((m, d), lambda i: (i, 0))
    st = pl.BlockSpec((nb, n_state), fix2)
    rowp = pl.BlockSpec((1, n_state), fix2)
    vecd = pl.BlockSpec((1, d), fix2)
    return pl.pallas_call(
        functools.partial(_s5_kernel, tc=tc, nb=nb),
        grid=(rows // m,),
        in_specs=[blk, st, st, rowp, rowp, pl.BlockSpec(bblk.shape, fix3), pl.BlockSpec(cblk.shape, fix3),
                  vecd, pl.BlockSpec(w_glu.shape, fix2), vecd],
        out_specs=[blk, st, st],
        out_shape=[jax.ShapeDtypeStruct((rows, d), BF16), jax.ShapeDtypeStruct((nb, n_state), F32),
                   jax.ShapeDtypeStruct((nb, n_state), F32)],
        scratch_shapes=[pltpu.VMEM((nb, n_state), F32), pltpu.VMEM((nb, n_state), F32),
                        pltpu.VMEM((m, 2 * S5_GB * S5_STATE), F32), pltpu.VMEM((m, 2 * S5_GB * S5_STATE), F32),
                        pltpu.VMEM((m, d), F32)],
        compiler_params=_params("arbitrary"),
        name="s5",
    )(u_tm, h0_re, h0_im, lb_re, lb_im, bblk, cblk, dskip, w_glu, b_glu)


def _merge_kernel(x_ref, att_ref, s5_ref, ga_ref, gs_ref, wa_ref, ws_ref, wo_ref, gf_ref, wq_ref, sk_ref,
                  x1_ref, xn_ref, sc_ref):
    pa = jnp.dot(att_ref[...], wa_ref[...], preferred_element_type=F32)
    ps = jnp.dot(s5_ref[...], ws_ref[...], preferred_element_type=F32)
    merged = jax.nn.sigmoid(ga_ref[...]) * pa + jax.nn.sigmoid(gs_ref[...]) * ps
    x1 = x_ref[...] + jnp.dot(merged.astype(BF16), wo_ref[...], preferred_element_type=F32)
    x1_ref[...] = x1
    xn = _rms(x1, gf_ref[...]).astype(BF16)
    xn_ref[...] = xn
    qt = lax.dot_general(wq_ref[...], xn, _NT, preferred_element_type=F32).astype(BF16)
    for i in range(2 * PEER_HEADS):
        rows = slice(i * PEER_HALF, (i + 1) * PEER_HALF)
        sc_ref[i * PEER_KEYS:(i + 1) * PEER_KEYS, :] = jnp.dot(sk_ref[i], qt[rows, :], preferred_element_type=F32)


def _merge(x, att, s5o, g_att, g_s5, w_att, w_s5, w_out, g_ffn, wq_t, sk, tm):
    t, d = x.shape
    row = pl.BlockSpec((tm, d), lambda i: (i, 0))
    fix2 = lambda i: (0, 0)
    wspec = pl.BlockSpec((d, d), fix2)
    n_sc = 2 * PEER_HEADS * PEER_KEYS
    return pl.pallas_call(
        _merge_kernel,
        grid=(t // tm,),
        in_specs=[row, row, row, row, row, wspec, wspec, wspec, pl.BlockSpec((1, d), fix2),
                  pl.BlockSpec(wq_t.shape, fix2), pl.BlockSpec(sk.shape, lambda i: (0, 0, 0))],
        out_specs=[row, row, pl.BlockSpec((n_sc, tm), lambda i: (0, i))],
        out_shape=[jax.ShapeDtypeStruct((t, d), F32), jax.ShapeDtypeStruct((t, d), BF16),
                   jax.ShapeDtypeStruct((n_sc, t), F32)],
        compiler_params=_params("parallel"),
        name="merge",
    )(x, att, s5o, g_att, g_s5, w_att, w_s5, w_out, g_ffn, wq_t, sk)


def _top_values(s, want_rank):
    rows = lax.broadcasted_iota(jnp.int32, (PEER_TOPK, s.shape[1]), 0)
    vals = jnp.full((PEER_TOPK, s.shape[1]), NEG_INF, F32)
    rank = jnp.full(s.shape, NO_RANK, F32)
    work = s
    for a in range(PEER_TOPK):
        m = jnp.max(work, axis=0, keepdims=True)
        hit = work == m
        vals = jnp.where(rows == a, m, vals)
        if want_rank:
            rank = jnp.where(hit, float(a), rank)
        work = jnp.where(hit, NEG_INF, work)
    return vals, rank


def _route_kernel(sc_ref, a_ref, c_ref, r2_ref, e2_ref):
    s1 = sc_ref[0:PEER_KEYS, :]
    s2 = sc_ref[PEER_KEYS:2 * PEER_KEYS, :]
    v1, _ = _top_values(s1, False)
    v2, rank2 = _top_values(s2, True)
    cand = [v1[0:1, :] + v2]
    for a in range(1, PEER_TOPK):
        cand.append(v1[a:a + 1, :] + v2[0:SUBLANES, :])
    cand = jnp.concatenate(cand, axis=0)
    theta = None
    for _ in range(PEER_TOPK):
        theta = jnp.max(cand, axis=0, keepdims=True)
        cand = jnp.where(cand == theta, NEG_INF, cand)
    m1, m2 = v1[0:1, :], v2[0:1, :]
    e1 = jnp.exp(s1 - m1)
    e2v = jnp.exp(v2 - m2)
    cnt = jnp.zeros_like(s1)
    g2 = jnp.zeros_like(s1)
    for b in range(PEER_TOPK):
        sel = (s1 + v2[b:b + 1, :]) >= theta
        cnt = cnt + jnp.where(sel, 1.0, 0.0)
        g2 = g2 + jnp.where(sel, e2v[b:b + 1, :], 0.0)
    z = jnp.sum(e1 * g2, axis=0, keepdims=True)
    a_ref[...] = e1 / z
    c_ref[...] = cnt
    r2_ref[...] = rank2
    e2_ref[...] = jnp.exp(s2 - m2)


def _route(sc, tb):
    n_sc, t = sc.shape
    n_out = PEER_HEADS * PEER_KEYS
    ospec = pl.BlockSpec((PEER_KEYS, tb), lambda i, h: (h, i))
    oshape = jax.ShapeDtypeStruct((n_out, t), F32)
    return pl.pallas_call(
        _route_kernel,
        grid=(t // tb, PEER_HEADS),
        in_specs=[pl.BlockSpec((2 * PEER_KEYS, tb), lambda i, h: (h, i))],
        out_specs=[ospec] * 4,
        out_shape=[oshape] * 4,
        compiler_params=_params("parallel", "parallel"),
        name="route",
    )(sc)


def _experts_kernel(xn_ref, x1_ref, u_ref, vt_ref, a_ref, c_ref, r2_ref, e2_ref, gf_ref, y_ref, acc, wt):
    e = pl.program_id(1)
    eb = u_ref.shape[0]
    n_i = eb // PEER_KEYS

    @pl.when(e == 0)
    def _():
        acc[...] = jnp.zeros_like(acc)

    act = lax.dot_general(u_ref[...], xn_ref[...], _NT, preferred_element_type=F32)
    for il in range(n_i):
        i_glob = e * n_i + il
        rows = slice(il * PEER_KEYS, (il + 1) * PEER_KEYS)
        g = jnp.zeros((PEER_KEYS, act.shape[1]), F32)
        for h in range(PEER_HEADS):
            hk = slice(h * PEER_KEYS, (h + 1) * PEER_KEYS)
            c_row = c_ref[pl.ds(h * PEER_KEYS + i_glob, 1), :]
            a_row = a_ref[pl.ds(h * PEER_KEYS + i_glob, 1), :]
            g = g + jnp.where(r2_ref[hk, :] < c_row, e2_ref[hk, :], 0.0) * a_row
        wt[rows, :] = (g * _gelu(act[rows, :])).astype(BF16)
    acc[...] += jnp.dot(vt_ref[...], wt[...], preferred_element_type=F32)

    @pl.when(e == pl.num_programs(1) - 1)
    def _():
        y_ref[...] = _rms(x1_ref[...] + acc[...].T, gf_ref[...])


def _experts(xn, x1, u, vt, a, c, r2, e2, g_final, tb, eb):
    t, d = x1.shape
    n_exp = u.shape[0]
    n_r = a.shape[0]
    tok = pl.BlockSpec((tb, d), lambda i, e: (i, 0))
    rt = pl.BlockSpec((n_r, tb), lambda i, e: (0, i))
    return pl.pallas_call(
        _experts_kernel,
        grid=(t // tb, n_exp // eb),
        in_specs=[tok, tok, pl.BlockSpec((eb, d), lambda i, e: (e, 0)), pl.BlockSpec((d, eb), lambda i, e: (0, e)),
                  rt, rt, rt, rt, pl.BlockSpec((1, d), lambda i, e: (0, 0))],
        out_specs=tok,
        out_shape=jax.ShapeDtypeStruct((t, d), F32),
        scratch_shapes=[pltpu.VMEM((d, tb), F32), pltpu.VMEM((eb, tb), BF16)],
        compiler_params=_params("parallel", "arbitrary"),
        name="experts",
    )(xn, x1, u, vt, a, c, r2, e2, g_final)


def _block_diag_b(bb_re, bb_im):
    c = bb_re.shape[0]
    eye = jnp.eye(S5_GB, dtype=F32)

    def one(bb):
        x = bb.reshape(c, N_GB, S5_GB, S5_STATE)
        x = jnp.transpose(x, (1, 2, 0, 3))
        x = x[:, :, :, None, :] * eye[None, :, None, :, None]
        return x.reshape(N_GB, S5_GB * c, S5_GB * S5_STATE)

    return jnp.concatenate([one(bb_re), one(bb_im)], axis=-1).astype(BF16)


def _block_diag_c(c_re, c_im):
    c = c_re.shape[1]
    eye = jnp.eye(S5_GB, dtype=F32)

    def one(cc):
        x = cc.reshape(N_GB, S5_GB, c, S5_STATE)
        x = jnp.transpose(x, (0, 1, 3, 2))
        x = x[:, :, :, None, :] * eye[None, :, None, :, None]
        return x.reshape(N_GB, S5_GB * S5_STATE, S5_GB * c)

    return jnp.concatenate([one(c_re), one(-c_im)], axis=1).astype(BF16)


def _tile(t, pref):
    return pref if t % pref == 0 else t


def kernel(x_prompt, x_sample, cache_k, cache_v, state_s5_re, state_s5_im, page_table, norm_mix_g, w_in,
           lambda_q1, lambda_k1, lambda_q2, lambda_k2, subln_g, s5_a_re, s5_a_im, s5_log_step, s5_b_re, s5_b_im,
           s5_c_re, s5_c_im, s5_d, w_glu, b_glu, w_proj_att, w_proj_s5, w_out, norm_ffn_g, peer_w_query,
           peer_sub_keys, peer_u, peer_v, norm_final_g):
    depth = w_in.shape[0]
    nbp, seq, d = x_prompt.shape
    nbs, n_new, _ = x_sample.shape
    n_state = S5_GROUPS * S5_STATE
    xp = x_prompt.reshape(nbp * seq, d)
    xs = x_sample.reshape(nbs * n_new, d)
    g_final = norm_final_g.reshape(1, d)
    outs = [[] for _ in range(8)]
    for l in range(depth):
        lam_init = 0.8 - 0.6 * math.exp(-0.3 * l)
        lams = [v[l].reshape(1, HEAD_DIM) for v in (lambda_q1, lambda_k1, lambda_q2, lambda_k2)]
        sg = subln_g[l].reshape(1, HEAD_W)
        sg_tiled = jnp.tile(sg, (1, N_HEADS))
        w_in_b = w_in[l].astype(BF16)
        g_mix = norm_mix_g[l].reshape(1, d)
        step_row = jnp.repeat(s5_log_step[l], S5_STATE).reshape(1, n_state)
        b_t = lambda b: jnp.transpose(b.reshape(n_state, S5_CH))
        lb_re, lb_im, bb_re, bb_im = _s5prep(s5_a_re[l].reshape(1, n_state), s5_a_im[l].reshape(1, n_state),
                                             step_row, b_t(s5_b_re[l]), b_t(s5_b_im[l]))
        bblk = _block_diag_b(bb_re, bb_im)
        cblk = _block_diag_c(s5_c_re[l], s5_c_im[l])
        dskip = s5_d[l].reshape(1, d)
        w_glu_b = w_glu[l].astype(BF16)
        bg = b_glu[l].reshape(1, d)
        w_att_b, w_s5_b, w_out_b = (w[l].astype(BF16) for w in (w_proj_att, w_proj_s5, w_out))
        g_ffn = norm_ffn_g[l].reshape(1, d)
        wq_t = jnp.transpose(peer_w_query[l]).astype(BF16)
        sk = peer_sub_keys[l].reshape(2 * PEER_HEADS, PEER_KEYS, PEER_HALF).astype(BF16)
        u_b = peer_u[l].astype(BF16)
        vt_b = jnp.transpose(peer_v[l]).astype(BF16)
        ck = cache_k[l].reshape(cache_k.shape[1], cache_k.shape[2], d)
        cv = cache_v[l].reshape(cache_v.shape[1], cache_v.shape[2], d)

        def mixer_tail(x, att, u, g_att, g_s5, h0_re, h0_im, nb, length, tc):
            u_tm = jnp.transpose(u.reshape(nb, length, d), (1, 0, 2)).reshape(length * nb, d)
            s5o_tm, hr, hi = _s5(u_tm, h0_re, h0_im, lb_re, lb_im, bblk, cblk, dskip, w_glu_b, bg, nb, tc)
            s5o = jnp.transpose(s5o_tm.reshape(length, nb, d), (1, 0, 2)).reshape(nb * length, d)
            t = x.shape[0]
            x1, xn, sc = _merge(x, att, s5o, g_att, g_s5, w_att_b, w_s5_b, w_out_b, g_ffn, wq_t, sk, _tile(t, 256))
            a, c, r2, e2 = _route(sc, _tile(t, 512))
            y = _experts(xn, x1, u_b, vt_b, a, c, r2, e2, g_final, _tile(t, 512), 1024)
            return y, hr, hi

        q, k, v, u, g_att, g_s5 = _inproj(xp, g_mix, w_in_b, _tile(xp.shape[0], 256))
        att = _pattn(q, k, v, lams, sg, nbp, seq, _tile(seq, 256), lam_init)
        zeros = jnp.zeros((nbp, n_state), F32)
        xp, hr_p, hi_p = mixer_tail(xp, att, u, g_att, g_s5, zeros, zeros, nbp, seq, _tile(seq, 64))
        qs, ks, vs, us, gs_att, gs_s5 = _inproj(xs, g_mix, w_in_b, _tile(xs.shape[0], 256))
        shp = (nbs, n_new, d)
        att_s = _sattn(page_table, qs.reshape(shp), ks.reshape(shp), vs.reshape(shp), ck, cv, lams, sg_tiled,
                       lam_init).reshape(nbs * n_new, d)
        xs, hr_s, hi_s = mixer_tail(xs, att_s, us, gs_att, gs_s5, state_s5_re[l].reshape(nbs, n_state),
                                    state_s5_im[l].reshape(nbs, n_state), nbs, n_new, n_new)
        kv_p = (nbp, seq, N_HEADS, HEAD_W)
        kv_s = (nbs, n_new, N_HEADS, HEAD_W)
        st = (S5_GROUPS, S5_STATE)
        for lst, val in zip(outs, (k.reshape(kv_p), v.reshape(kv_p), hr_p.reshape(nbp, *st), hi_p.reshape(nbp, *st),
                                   ks.reshape(kv_s), vs.reshape(kv_s), hr_s.reshape(nbs, *st),
                                   hi_s.reshape(nbs, *st))):
            lst.append(val)
    assert depth == 1
    return (xp.reshape(nbp, seq, d), xs.reshape(nbs, n_new, d)) + tuple(jnp.stack(o) for o in outs)
```

```python
import functools
import math

import jax
import jax.numpy as jnp
from jax import lax
from jax.experimental import pallas as pl
from jax.experimental.pallas import tpu as pltpu

F32 = jnp.float32
BF16 = jnp.bfloat16

LANES = 128
SUBLANES = 8
VMEM_LIMIT = 56 * 1024 * 1024

RMS_EPS = 1e-6
N_HEADS = 8
HEAD_DIM = 64
HEAD_W = 2 * HEAD_DIM
S5_GROUPS = 64
S5_CH = 16
S5_STATE = 64
S5_GB = 8
N_GB = S5_GROUPS // S5_GB
PEER_HEADS = 8
PEER_KEYS = 128
PEER_HALF = 128
PEER_TOPK = 16
NEG_INF = float("-inf")
NO_RANK = 99.0

_NT = (((1,), (1,)), ((), ()))


def _params(*sem):
    return pltpu.CompilerParams(dimension_semantics=sem, vmem_limit_bytes=VMEM_LIMIT)


def _rms(x, g):
    return x * lax.rsqrt(jnp.mean(x * x, axis=-1, keepdims=True) + RMS_EPS) * g


def _gelu(x):
    return 0.5 * x * (1.0 + lax.erf(x * math.sqrt(0.5)))


def _lam(lq1, lk1, lq2, lk2, lam_init):
    a = jnp.sum(lq1[...] * lk1[...], axis=-1, keepdims=True)
    b = jnp.sum(lq2[...] * lk2[...], axis=-1, keepdims=True)
    return jnp.exp(a) - jnp.exp(b) + lam_init


def _inproj_kernel(x_ref, g_ref, w_ref, q_ref, k_ref, v_ref, u_ref, ga_ref, gs_ref, *, d, q_scale):
    h = _rms(x_ref[...], g_ref[...]).astype(BF16)
    outs = (q_ref, k_ref, v_ref, u_ref, ga_ref, gs_ref)
    for j, o in enumerate(outs):
        r = jnp.dot(h, w_ref[:, j * d:(j + 1) * d], preferred_element_type=F32)
        if j == 0:
            r = r * q_scale
        o[...] = r.astype(o.dtype)


def _inproj(x, g, w, tm):
    t, d = x.shape
    row = lambda i: (i, 0)
    fix = lambda i: (0, 0)
    spec = pl.BlockSpec((tm, d), row)
    return pl.pallas_call(
        functools.partial(_inproj_kernel, d=d, q_scale=HEAD_DIM ** -0.5),
        grid=(t // tm,),
        in_specs=[spec, pl.BlockSpec((1, d), fix), pl.BlockSpec(w.shape, fix)],
        out_specs=[spec] * 6,
        out_shape=[jax.ShapeDtypeStruct((t, d), BF16)] + [jax.ShapeDtypeStruct((t, d), F32)] * 5,
        compiler_params=_params("parallel"),
        name="inproj",
    )(x, g, w)


def _split_maps(q):
    lane = lax.broadcasted_iota(jnp.int32, q.shape, 1)
    zero = jnp.zeros_like(q)
    return jnp.where(lane < HEAD_DIM, q, zero), jnp.where(lane >= HEAD_DIM, q, zero)


def _pattn_kernel(q_ref, k_ref, v_ref, lq1, lk1, lq2, lk2, sg_ref, o_ref, *, tq, lam_init):
    lam = _lam(lq1, lk1, lq2, lk2, lam_init)
    kb = k_ref[...].astype(BF16)
    vb = v_ref[...].astype(BF16)
    r = lax.broadcasted_iota(jnp.int32, (tq, tq), 0)
    c = lax.broadcasted_iota(jnp.int32, (tq, tq), 1)
    causal = c <= r
    for i in range(q_ref.shape[0] // tq):
        lo = i * tq
        qs = _split_maps(q_ref[lo:lo + tq, :])
        probs, sums = [], []
        for qm in qs:
            sd = jnp.where(causal, lax.dot_general(qm, kb[lo:lo + tq], _NT, preferred_element_type=F32), NEG_INF)
            mx = jnp.max(sd, axis=-1, keepdims=True)
            if i:
                so = lax.dot_general(qm, kb[:lo], _NT, preferred_element_type=F32)
                mx = jnp.maximum(mx, jnp.max(so, axis=-1, keepdims=True))
                po = jnp.exp(so - mx)
            pd = jnp.exp(sd - mx)
            tot = jnp.sum(pd, axis=-1, keepdims=True)
            if i:
                tot = tot + jnp.sum(po, axis=-1, keepdims=True)
            probs.append((po if i else None, pd))
            sums.append(tot)
        pv = jnp.dot(jnp.concatenate([probs[0][1], probs[1][1]], axis=0).astype(BF16), vb[lo:lo + tq],
                     preferred_element_type=F32)
        if i:
            pv = pv + jnp.dot(jnp.concatenate([probs[0][0], probs[1][0]], axis=0).astype(BF16), vb[:lo],
                              preferred_element_type=F32)
        att = pv[:tq] / sums[0] - lam * (pv[tq:] / sums[1])
        o_ref[lo:lo + tq, :] = (_rms(att, sg_ref[...]) * (1.0 - lam_init)).astype(o_ref.dtype)


def _pattn(q, k, v, lams, sg, nb, s, tq, lam_init):
    t, d = k.shape
    vec = pl.BlockSpec((1, HEAD_DIM), lambda b, h: (0, 0))
    spec = pl.BlockSpec((s, HEAD_W), lambda b, h: (b, h))
    return pl.pallas_call(
        functools.partial(_pattn_kernel, tq=tq, lam_init=lam_init),
        grid=(nb, N_HEADS),
        in_specs=[spec, spec, spec, vec, vec, vec, vec, pl.BlockSpec((1, HEAD_W), lambda b, h: (0, 0))],
        out_specs=spec,
        out_shape=jax.ShapeDtypeStruct((t, d), BF16),
        compiler_params=_params("parallel", "parallel"),
        name="pattn",
    )(q, k, v, *lams, sg)


def _sattn_kernel(pt_ref, q_ref, kn_ref, vn_ref, ck_ref, cv_ref, lq1, lk1, lq2, lk2, sg_ref, o_ref,
                  kbuf, vbuf, sem, *, n_pages, page, n_new, lam_init):
    s = pl.program_id(0)
    ns = pl.num_programs(0)
    prow = page * N_HEADS
    past = n_pages * page
    new_rows = n_new * N_HEADS

    def page_copies(seq, slot):
        out = []
        for j in range(n_pages):
            p = pt_ref[seq, j]
            out.append(pltpu.make_async_copy(ck_ref.at[p], kbuf.at[slot, pl.ds(j * prow, prow)], sem.at[slot, 0]))
            out.append(pltpu.make_async_copy(cv_ref.at[p], vbuf.at[slot, pl.ds(j * prow, prow)], sem.at[slot, 1]))
        return out

    @pl.when(s == 0)
    def _():
        for c in page_copies(0, 0):
            c.start()
        pad = kbuf.shape[1] - past * N_HEADS
        z = jnp.zeros((2, pad, HEAD_W), F32)
        kbuf[:, pl.ds(past * N_HEADS, pad), :] = z
        vbuf[:, pl.ds(past * N_HEADS, pad), :] = z

    slot = s % 2

    @pl.when(s + 1 < ns)
    def _():
        for c in page_copies(s + 1, 1 - slot):
            c.start()

    kbuf[slot, pl.ds(past * N_HEADS, new_rows), :] = kn_ref[...]
    vbuf[slot, pl.ds(past * N_HEADS, new_rows), :] = vn_ref[...]
    for c in page_copies(s, slot):
        c.wait()

    lam = _lam(lq1, lk1, lq2, lk2, lam_init)
    nq = n_new * N_HEADS
    q1, q2 = _split_maps(q_ref[...])
    qx = jnp.concatenate([q1, q2], axis=0)
    r = lax.broadcasted_iota(jnp.int32, (2 * nq, LANES), 0)
    c = lax.broadcasted_iota(jnp.int32, (2 * nq, LANES), 1)
    same_head = (c % N_HEADS) == (r % N_HEADS)
    bias_past = jnp.where(same_head, 0.0, NEG_INF)
    bias_new = jnp.where(same_head & (c // N_HEADS <= (r % nq) // N_HEADS), 0.0, NEG_INF)
    rows = kbuf.shape[1]
    chunk = 16 * LANES
    starts = list(range(0, rows, chunk))
    pieces = []
    for c0 in starts:
        w = min(chunk, rows - c0)
        kc = kbuf[slot, pl.ds(c0, w), :].astype(BF16)
        sc = lax.dot_general(qx, kc, _NT, preferred_element_type=F32)
        n_past = max(0, min(w, past * N_HEADS - c0)) // LANES
        bias = [bias_past] * n_past + [bias_new] * (w // LANES - n_past)
        pieces.append(sc + jnp.concatenate(bias, axis=1))
    sc = jnp.concatenate(pieces, axis=1)
    pr = jnp.exp(sc - jnp.max(sc, axis=-1, keepdims=True))
    tot = jnp.sum(pr, axis=-1, keepdims=True)
    pb = pr.astype(BF16)
    acc = jnp.zeros((2 * nq, HEAD_W), F32)
    for c0 in starts:
        w = min(chunk, rows - c0)
        vc = vbuf[slot, pl.ds(c0, w), :].astype(BF16)
        acc = acc + jnp.dot(pb[:, c0:c0 + w], vc, preferred_element_type=F32)
    out = acc / tot
    att = out[:nq] - lam * out[nq:]
    o_ref[...] = (_rms(att, sg_ref[...]) * (1.0 - lam_init)).astype(o_ref.dtype)


def _sattn(page_table, q, kn, vn, cache_k, cache_v, lams, sg, lam_init):
    ns, new_rows, _ = q.shape
    n_new = new_rows // N_HEADS
    n_pages = page_table.shape[1]
    page = cache_k.shape[1] // N_HEADS
    assert new_rows <= LANES
    rows = n_pages * page * N_HEADS + LANES
    vec = pl.BlockSpec((1, HEAD_DIM), lambda s, pt: (0, 0))
    newspec = pl.BlockSpec((None, new_rows, HEAD_W), lambda s, pt: (s, 0, 0))
    anyspec = pl.BlockSpec(memory_space=pl.ANY)
    grid_spec = pltpu.PrefetchScalarGridSpec(
        num_scalar_prefetch=1,
        grid=(ns,),
        in_specs=[newspec, newspec, newspec, anyspec, anyspec, vec, vec, vec, vec,
                  pl.BlockSpec((1, HEAD_W), lambda s, pt: (0, 0))],
        out_specs=newspec,
        scratch_shapes=[pltpu.VMEM((2, rows, HEAD_W), F32), pltpu.VMEM((2, rows, HEAD_W), F32),
                        pltpu.SemaphoreType.DMA((2, 2))],
    )
    return pl.pallas_call(
        functools.partial(_sattn_kernel, n_pages=n_pages, page=page, n_new=n_new, lam_init=lam_init),
        grid_spec=grid_spec,
        out_shape=jax.ShapeDtypeStruct((ns, new_rows, HEAD_W), BF16),
        compiler_params=_params("arbitrary"),
        name="sattn",
    )(page_table, q, kn, vn, cache_k, cache_v, *lams, sg)


def _s5prep_kernel(are_ref, aim_ref, ls_ref, bre_ref, bim_ref, lbr_ref, lbi_ref, bbr_ref, bbi_ref):
    a_re, a_im = are_ref[...], aim_ref[...]
    dt = jnp.exp(ls_ref[...])
    mag = jnp.exp(a_re * dt)
    ang = a_im * dt
    lb_re, lb_im = mag * jnp.cos(ang), mag * jnp.sin(ang)
    num_re, num_im = lb_re - 1.0, lb_im
    den = a_re * a_re + a_im * a_im
    f_re = (num_re * a_re + num_im * a_im) / den
    f_im = (num_im * a_re - num_re * a_im) / den
    b_re, b_im = bre_ref[...], bim_ref[...]
    lbr_ref[...] = lb_re
    lbi_ref[...] = lb_im
    bbr_ref[...] = f_re * b_re - f_im * b_im
    bbi_ref[...] = f_re * b_im + f_im * b_re


def _s5prep(a_re, a_im, log_step, b_re, b_im):
    n = a_re.shape[1]
    c = b_re.shape[0]
    row = jax.ShapeDtypeStruct((1, n), F32)
    mat = jax.ShapeDtypeStruct((c, n), F32)
    return pl.pallas_call(_s5prep_kernel, out_shape=[row, row, mat, mat], name="s5prep")(
        a_re, a_im, log_step, b_re, b_im)


def _s5_kernel(u_ref, h0r_ref, h0i_ref, lbr_ref, lbi_ref, bb_ref, cc_ref, dsk_ref, wg_ref, bg_ref,
               o_ref, hr_ref, hi_ref, st_re, st_im, bu, hb, y, *, tc, nb):
    step = pl.program_id(0)
    gw = S5_GB * S5_STATE

    @pl.when(step == 0)
    def _():
        st_re[...] = h0r_ref[...]
        st_im[...] = h0i_ref[...]

    u = u_ref[...]
    ub = u.astype(BF16)
    for gb in range(N_GB):
        bu[...] = jnp.dot(ub[:, gb * LANES:(gb + 1) * LANES], bb_ref[gb], preferred_element_type=F32)
        cols = pl.ds(gb * gw, gw)
        lr = jnp.broadcast_to(lbr_ref[:, cols], (SUBLANES, gw))
        li = jnp.broadcast_to(lbi_ref[:, cols], (SUBLANES, gw))
        for bt in range(nb // SUBLANES):
            rows = pl.ds(bt * SUBLANES, SUBLANES)

            def tstep(t, carry):
                hr, hi = carry
                r0 = pl.multiple_of(t * nb + bt * SUBLANES, SUBLANES)
                bur = bu[pl.ds(r0, SUBLANES), 0:gw]
                bui = bu[pl.ds(r0, SUBLANES), gw:2 * gw]
                nr = lr * hr - li * hi + bur
                ni = lr * hi + li * hr + bui
                hb[pl.ds(r0, SUBLANES), 0:gw] = nr
                hb[pl.ds(r0, SUBLANES), gw:2 * gw] = ni
                return nr, ni

            hr, hi = lax.fori_loop(0, tc, tstep, (st_re[rows, cols], st_im[rows, cols]))
            st_re[rows, cols] = hr
            st_im[rows, cols] = hi
        y[:, gb * LANES:(gb + 1) * LANES] = jnp.dot(hb[...].astype(BF16), cc_ref[gb],
                                                      preferred_element_type=F32)
    yy = y[...] + dsk_ref[...] * u
    z = _gelu(yy)
    gate = jnp.dot(z.astype(BF16), wg_ref[...], preferred_element_type=F32) + bg_ref[...]
    o_ref[...] = (z * jax.nn.sigmoid(gate)).astype(o_ref.dtype)

    @pl.when(step == pl.num_programs(0) - 1)
    def _():
        hr_ref[...] = st_re[...]
        hi_ref[...] = st_im[...]


def _s5(u_tm, h0_re, h0_im, lb_re, lb_im, bblk, cblk, dskip, w_glu, b_glu, nb, tc):
    rows, d = u_tm.shape
    n_state = h0_re.shape[1]
    m = tc * nb
    fix2 = lambda i: (0, 0)
    fix3 = lambda i: (0, 0, 0)
    blk = pl.BlockSpec((m, d), lambda i: (i, 0))
    st = pl.BlockSpec((nb, n_state), fix2)
    rowp = pl.BlockSpec((1, n_state), fix2)
    vecd = pl.BlockSpec((1, d), fix2)
    return pl.pallas_call(
        functools.partial(_s5_kernel, tc=tc, nb=nb),
        grid=(rows // m,),
        in_specs=[blk, st, st, rowp, rowp, pl.BlockSpec(bblk.shape, fix3), pl.BlockSpec(cblk.shape, fix3),
                  vecd, pl.BlockSpec(w_glu.shape, fix2), vecd],
        out_specs=[blk, st, st],
        out_shape=[jax.ShapeDtypeStruct((rows, d), BF16), jax.ShapeDtypeStruct((nb, n_state), F32),
                   jax.ShapeDtypeStruct((nb, n_state), F32)],
        scratch_shapes=[pltpu.VMEM((nb, n_state), F32), pltpu.VMEM((nb, n_state), F32),
                        pltpu.VMEM((m, 2 * S5_GB * S5_STATE), F32), pltpu.VMEM((m, 2 * S5_GB * S5_STATE), F32),
                        pltpu.VMEM((m, d), F32)],
        compiler_params=_params("arbitrary"),
        name="s5",
    )(u_tm, h0_re, h0_im, lb_re, lb_im, bblk, cblk, dskip, w_glu, b_glu)


def _merge_kernel(x_ref, att_ref, s5_ref, ga_ref, gs_ref, wa_ref, ws_ref, wo_ref, gf_ref, wq_ref, sk_ref,
                  x1_ref, xn_ref, sc_ref):
    pa = jnp.dot(att_ref[...], wa_ref[...], preferred_element_type=F32)
    ps = jnp.dot(s5_ref[...], ws_ref[...], preferred_element_type=F32)
    merged = jax.nn.sigmoid(ga_ref[...]) * pa + jax.nn.sigmoid(gs_ref[...]) * ps
    x1 = x_ref[...] + jnp.dot(merged.astype(BF16), wo_ref[...], preferred_element_type=F32)
    x1_ref[...] = x1
    xn = _rms(x1, gf_ref[...]).astype(BF16)
    xn_ref[...] = xn
    qt = lax.dot_general(wq_ref[...], xn, _NT, preferred_element_type=F32).astype(BF16)
    for i in range(2 * PEER_HEADS):
        rows = slice(i * PEER_HALF, (i + 1) * PEER_HALF)
        sc_ref[i * PEER_KEYS:(i + 1) * PEER_KEYS, :] = jnp.dot(sk_ref[i], qt[rows, :], preferred_element_type=F32)


def _merge(x, att, s5o, g_att, g_s5, w_att, w_s5, w_out, g_ffn, wq_t, sk, tm):
    t, d = x.shape
    row = pl.BlockSpec((tm, d), lambda i: (i, 0))
    fix2 = lambda i: (0, 0)
    wspec = pl.BlockSpec((d, d), fix2)
    n_sc = 2 * PEER_HEADS * PEER_KEYS
    return pl.pallas_call(
        _merge_kernel,
        grid=(t // tm,),
        in_specs=[row, row, row, row, row, wspec, wspec, wspec, pl.BlockSpec((1, d), fix2),
                  pl.BlockSpec(wq_t.shape, fix2), pl.BlockSpec(sk.shape, lambda i: (0, 0, 0))],
        out_specs=[row, row, pl.BlockSpec((n_sc, tm), lambda i: (0, i))],
        out_shape=[jax.ShapeDtypeStruct((t, d), F32), jax.ShapeDtypeStruct((t, d), BF16),
                   jax.ShapeDtypeStruct((n_sc, t), F32)],
        compiler_params=_params("parallel"),
        name="merge",
    )(x, att, s5o, g_att, g_s5, w_att, w_s5, w_out, g_ffn, wq_t, sk)


def _top_values(s, want_rank):
    rows = lax.broadcasted_iota(jnp.int32, (PEER_TOPK, s.shape[1]), 0)
    vals = jnp.full((PEER_TOPK, s.shape[1]), NEG_INF, F32)
    rank = jnp.full(s.shape, NO_RANK, F32)
    work = s
    for a in range(PEER_TOPK):
        m = jnp.max(work, axis=0, keepdims=True)
        hit = work == m
        vals = jnp.where(rows == a, m, vals)
        if want_rank:
            rank = jnp.where(hit, float(a), rank)
        work = jnp.where(hit, NEG_INF, work)
    return vals, rank


def _route_kernel(sc_ref, a_ref, c_ref, r2_ref, e2_ref):
    s1 = sc_ref[0:PEER_KEYS, :]
    s2 = sc_ref[PEER_KEYS:2 * PEER_KEYS, :]
    v1, _ = _top_values(s1, False)
    v2, rank2 = _top_values(s2, True)
    cand = [v1[0:1, :] + v2]
    for a in range(1, PEER_TOPK):
        cand.append(v1[a:a + 1, :] + v2[0:SUBLANES, :])
    cand = jnp.concatenate(cand, axis=0)
    theta = None
    for _ in range(PEER_TOPK):
        theta = jnp.max(cand, axis=0, keepdims=True)
        cand = jnp.where(cand == theta, NEG_INF, cand)
    m1, m2 = v1[0:1, :], v2[0:1, :]
    e1 = jnp.exp(s1 - m1)
    e2v = jnp.exp(v2 - m2)
    cnt = jnp.zeros_like(s1)
    g2 = jnp.zeros_like(s1)
    for b in range(PEER_TOPK):
        sel = (s1 + v2[b:b + 1, :]) >= theta
        cnt = cnt + jnp.where(sel, 1.0, 0.0)
        g2 = g2 + jnp.where(sel, e2v[b:b + 1, :], 0.0)
    z = jnp.sum(e1 * g2, axis=0, keepdims=True)
    a_ref[...] = e1 / z
    c_ref[...] = cnt
    r2_ref[...] = rank2.astype(r2_ref.dtype)
    e2_ref[...] = jnp.exp(s2 - m2).astype(e2_ref.dtype)


def _route(sc, tb):
    n_sc, t = sc.shape
    n_out = PEER_HEADS * PEER_KEYS
    ospec = pl.BlockSpec((PEER_KEYS, tb), lambda i, h: (h, i))
    return pl.pallas_call(
        _route_kernel,
        grid=(t // tb, PEER_HEADS),
        in_specs=[pl.BlockSpec((2 * PEER_KEYS, tb), lambda i, h: (h, i))],
        out_specs=[ospec] * 4,
        out_shape=[jax.ShapeDtypeStruct((n_out, t), dt) for dt in (F32, F32, BF16, BF16)],
        compiler_params=_params("parallel", "parallel"),
        name="route",
    )(sc)


def _experts_kernel(xn_ref, x1_ref, u_ref, vt_ref, a_ref, c_ref, r2_ref, e2_ref, gf_ref, y_ref, acc, act_even,
                    act_odd):
    e = pl.program_id(1)
    n_e = pl.num_programs(1) - 1
    eb, tb = u_ref.shape[0], xn_ref.shape[0]
    n_i = eb // PEER_KEYS
    per = 2

    @pl.when(e == 0)
    def _():
        acc[...] = jnp.zeros_like(acc)
        act_odd[...] = jnp.zeros_like(act_odd)

    def step(act_new, act_old):
        act_new[...] = lax.dot_general(u_ref[...], xn_ref[...], _NT, preferred_element_type=F32)
        i_base = jnp.maximum(e - 1, 0) * n_i
        zero = jnp.zeros((PEER_KEYS, tb), BF16)
        out = None
        for j in range(n_i // per):
            ws = []
            for il in range(j * per, (j + 1) * per):
                g = zero
                for h in range(PEER_HEADS):
                    hk = slice(h * PEER_KEYS, (h + 1) * PEER_KEYS)
                    c_row = c_ref[pl.ds(h * PEER_KEYS + i_base + il, 1), :].astype(BF16)
                    a_row = a_ref[pl.ds(h * PEER_KEYS + i_base + il, 1), :].astype(BF16)
                    g = g + jnp.where(r2_ref[hk, :] < c_row, e2_ref[hk, :], zero) * a_row
                ws.append(g * _gelu(act_old[il * PEER_KEYS:(il + 1) * PEER_KEYS, :]).astype(BF16))
            cols = slice(j * per * PEER_KEYS, (j + 1) * per * PEER_KEYS)
            part = jnp.dot(vt_ref[:, cols], jnp.concatenate(ws, axis=0), preferred_element_type=F32)
            out = part if out is None else out + part
        acc[...] += out

    @pl.when(e % 2 == 0)
    def _():
        step(act_even, act_odd)

    @pl.when(e % 2 == 1)
    def _():
        step(act_odd, act_even)

    @pl.when(e == n_e)
    def _():
        y_ref[...] = _rms(x1_ref[...] + acc[...].T, gf_ref[...])


def _experts(xn, x1, u, vt, a, c, r2, e2, g_final, tb, eb):
    t, d = x1.shape
    n_e = u.shape[0] // eb
    n_r = a.shape[0]
    tok = pl.BlockSpec((tb, d), lambda i, e: (i, 0))
    rt = pl.BlockSpec((n_r, tb), lambda i, e: (0, i))
    return pl.pallas_call(
        _experts_kernel,
        grid=(t // tb, n_e + 1),
        in_specs=[tok, tok, pl.BlockSpec((eb, d), lambda i, e: (jnp.minimum(e, n_e - 1), 0)),
                  pl.BlockSpec((d, eb), lambda i, e: (0, jnp.maximum(e - 1, 0))),
                  rt, rt, rt, rt, pl.BlockSpec((1, d), lambda i, e: (0, 0))],
        out_specs=tok,
        out_shape=jax.ShapeDtypeStruct((t, d), F32),
        scratch_shapes=[pltpu.VMEM((d, tb), F32), pltpu.VMEM((eb, tb), F32), pltpu.VMEM((eb, tb), F32)],
        compiler_params=_params("parallel", "arbitrary"),
        name="experts",
    )(xn, x1, u, vt, a, c, r2, e2, g_final)


def _block_diag_b(bb_re, bb_im):
    c = bb_re.shape[0]
    eye = jnp.eye(S5_GB, dtype=F32)

    def one(bb):
        x = bb.reshape(c, N_GB, S5_GB, S5_STATE)
        x = jnp.transpose(x, (1, 2, 0, 3))
        x = x[:, :, :, None, :] * eye[None, :, None, :, None]
        return x.reshape(N_GB, S5_GB * c, S5_GB * S5_STATE)

    return jnp.concatenate([one(bb_re), one(bb_im)], axis=-1).astype(BF16)


def _block_diag_c(c_re, c_im):
    c = c_re.shape[1]
    eye = jnp.eye(S5_GB, dtype=F32)

    def one(cc):
        x = cc.reshape(N_GB, S5_GB, c, S5_STATE)
        x = jnp.transpose(x, (0, 1, 3, 2))
        x = x[:, :, :, None, :] * eye[None, :, None, :, None]
        return x.reshape(N_GB, S5_GB * S5_STATE, S5_GB * c)

    return jnp.concatenate([one(c_re), one(-c_im)], axis=1).astype(BF16)


def _tile(t, pref):
    return pref if t % pref == 0 else t


def kernel(x_prompt, x_sample, cache_k, cache_v, state_s5_re, state_s5_im, page_table, norm_mix_g, w_in,
           lambda_q1, lambda_k1, lambda_q2, lambda_k2, subln_g, s5_a_re, s5_a_im, s5_log_step, s5_b_re, s5_b_im,
           s5_c_re, s5_c_im, s5_d, w_glu, b_glu, w_proj_att, w_proj_s5, w_out, norm_ffn_g, peer_w_query,
           peer_sub_keys, peer_u, peer_v, norm_final_g):
    depth = w_in.shape[0]
    nbp, seq, d = x_prompt.shape
    nbs, n_new, _ = x_sample.shape
    n_state = S5_GROUPS * S5_STATE
    xp = x_prompt.reshape(nbp * seq, d)
    xs = x_sample.reshape(nbs * n_new, d)
    g_final = norm_final_g.reshape(1, d)
    outs = [[] for _ in range(8)]
    for l in range(depth):
        lam_init = 0.8 - 0.6 * math.exp(-0.3 * l)
        lams = [v[l].reshape(1, HEAD_DIM) for v in (lambda_q1, lambda_k1, lambda_q2, lambda_k2)]
        sg = subln_g[l].reshape(1, HEAD_W)
        w_in_b = w_in[l].astype(BF16)
        g_mix = norm_mix_g[l].reshape(1, d)
        step_row = jnp.repeat(s5_log_step[l], S5_STATE).reshape(1, n_state)
        b_t = lambda b: jnp.transpose(b.reshape(n_state, S5_CH))
        lb_re, lb_im, bb_re, bb_im = _s5prep(s5_a_re[l].reshape(1, n_state), s5_a_im[l].reshape(1, n_state),
                                             step_row, b_t(s5_b_re[l]), b_t(s5_b_im[l]))
        bblk = _block_diag_b(bb_re, bb_im)
        cblk = _block_diag_c(s5_c_re[l], s5_c_im[l])
        dskip = s5_d[l].reshape(1, d)
        w_glu_b = w_glu[l].astype(BF16)
        bg = b_glu[l].reshape(1, d)
        w_att_b, w_s5_b, w_out_b = (w[l].astype(BF16) for w in (w_proj_att, w_proj_s5, w_out))
        g_ffn = norm_ffn_g[l].reshape(1, d)
        wq_t = jnp.transpose(peer_w_query[l]).astype(BF16)
        sk = peer_sub_keys[l].reshape(2 * PEER_HEADS, PEER_KEYS, PEER_HALF).astype(BF16)
        u_b = peer_u[l].astype(BF16)
        vt_b = jnp.transpose(peer_v[l]).astype(BF16)
        ck = cache_k[l].reshape(cache_k.shape[1], cache_k.shape[2] * N_HEADS, HEAD_W)
        cv = cache_v[l].reshape(cache_v.shape[1], cache_v.shape[2] * N_HEADS, HEAD_W)

        def mixer_tail(x, att, u, g_att, g_s5, h0_re, h0_im, nb, length, tc):
            u_tm = jnp.transpose(u.reshape(nb, length, d), (1, 0, 2)).reshape(length * nb, d)
            s5o_tm, hr, hi = _s5(u_tm, h0_re, h0_im, lb_re, lb_im, bblk, cblk, dskip, w_glu_b, bg, nb, tc)
            s5o = jnp.transpose(s5o_tm.reshape(length, nb, d), (1, 0, 2)).reshape(nb * length, d)
            t = x.shape[0]
            x1, xn, sc = _merge(x, att, s5o, g_att, g_s5, w_att_b, w_s5_b, w_out_b, g_ffn, wq_t, sk, _tile(t, 256))
            a, c, r2, e2 = _route(sc, _tile(t, 512))
            y = _experts(xn, x1, u_b, vt_b, a, c, r2, e2, g_final, _tile(t, 512), 1024)
            return y, hr, hi

        q, k, v, u, g_att, g_s5 = _inproj(xp, g_mix, w_in_b, _tile(xp.shape[0], 256))
        att = _pattn(q, k, v, lams, sg, nbp, seq, _tile(seq, 512), lam_init)
        zeros = jnp.zeros((nbp, n_state), F32)
        xp, hr_p, hi_p = mixer_tail(xp, att, u, g_att, g_s5, zeros, zeros, nbp, seq, _tile(seq, 64))
        qs, ks, vs, us, gs_att, gs_s5 = _inproj(xs, g_mix, w_in_b, _tile(xs.shape[0], 256))
        new_shp = (nbs, n_new * N_HEADS, HEAD_W)
        att_s = _sattn(page_table, qs.reshape(new_shp), ks.reshape(new_shp), vs.reshape(new_shp), ck, cv, lams, sg,
                       lam_init).reshape(nbs * n_new, d)
        xs, hr_s, hi_s = mixer_tail(xs, att_s, us, gs_att, gs_s5, state_s5_re[l].reshape(nbs, n_state),
                                    state_s5_im[l].reshape(nbs, n_state), nbs, n_new, n_new)
        kv_p = (nbp, seq, N_HEADS, HEAD_W)
        kv_s = (nbs, n_new, N_HEADS, HEAD_W)
        st = (S5_GROUPS, S5_STATE)
        for lst, val in zip(outs, (k.reshape(kv_p), v.reshape(kv_p), hr_p.reshape(nbp, *st), hi_p.reshape(nbp, *st),
                                   ks.reshape(kv_s), vs.reshape(kv_s), hr_s.reshape(nbs, *st),
                                   hi_s.reshape(nbs, *st))):
            lst.append(val)
    assert depth == 1
    return (xp.reshape(nbp, seq, d), xs.reshape(nbs, n_new, d)) + tuple(jnp.stack(o) for o in outs)
```

```python
import functools
import math

import jax
import jax.numpy as jnp
from jax import lax
from jax.experimental import pallas as pl
from jax.experimental.pallas import tpu as pltpu

F32 = jnp.float32
BF16 = jnp.bfloat16

LANES = 128
SUBLANES = 8
VMEM_LIMIT = 56 * 1024 * 1024

RMS_EPS = 1e-6
N_HEADS = 8
HEAD_DIM = 64
HEAD_W = 2 * HEAD_DIM
S5_GROUPS = 64
S5_CH = 16
S5_STATE = 64
S5_GB = 8
N_GB = S5_GROUPS // S5_GB
PEER_HEADS = 8
PEER_KEYS = 128
PEER_HALF = 128
PEER_TOPK = 16
NEG_INF = float("-inf")
NO_RANK = 99.0

_NT = (((1,), (1,)), ((), ()))


def _params(*sem):
    return pltpu.CompilerParams(dimension_semantics=sem, vmem_limit_bytes=VMEM_LIMIT)


def _rms(x, g):
    return x * lax.rsqrt(jnp.mean(x * x, axis=-1, keepdims=True) + RMS_EPS) * g


def _gelu(x):
    return 0.5 * x * (1.0 + lax.erf(x * math.sqrt(0.5)))


def _lam(lq1, lk1, lq2, lk2, lam_init):
    a = jnp.sum(lq1[...] * lk1[...], axis=-1, keepdims=True)
    b = jnp.sum(lq2[...] * lk2[...], axis=-1, keepdims=True)
    return jnp.exp(a) - jnp.exp(b) + lam_init


def _inproj_kernel(x_ref, g_ref, w_ref, q_ref, k_ref, v_ref, u_ref, ga_ref, gs_ref, *, d, q_scale):
    h = _rms(x_ref[...], g_ref[...]).astype(BF16)
    outs = (q_ref, k_ref, v_ref, u_ref, ga_ref, gs_ref)
    for j, o in enumerate(outs):
        r = jnp.dot(h, w_ref[:, j * d:(j + 1) * d], preferred_element_type=F32)
        if j == 0:
            r = r * q_scale
        o[...] = r.astype(o.dtype)


def _inproj(x, g, w, tm):
    t, d = x.shape
    row = lambda i: (i, 0)
    fix = lambda i: (0, 0)
    spec = pl.BlockSpec((tm, d), row)
    return pl.pallas_call(
        functools.partial(_inproj_kernel, d=d, q_scale=HEAD_DIM ** -0.5),
        grid=(t // tm,),
        in_specs=[spec, pl.BlockSpec((1, d), fix), pl.BlockSpec(w.shape, fix)],
        out_specs=[spec] * 6,
        out_shape=[jax.ShapeDtypeStruct((t, d), BF16)] + [jax.ShapeDtypeStruct((t, d), F32)] * 5,
        compiler_params=_params("parallel"),
        name="inproj",
    )(x, g, w)


def _split_maps(q):
    lane = lax.broadcasted_iota(jnp.int32, q.shape, 1)
    zero = jnp.zeros_like(q)
    return jnp.where(lane < HEAD_DIM, q, zero), jnp.where(lane >= HEAD_DIM, q, zero)


def _pattn_kernel(q_ref, k_ref, v_ref, lq1, lk1, lq2, lk2, sg_ref, o_ref, *, tq, lam_init):
    lam = _lam(lq1, lk1, lq2, lk2, lam_init)
    kb = k_ref[...].astype(BF16)
    vb = v_ref[...].astype(BF16)
    r = lax.broadcasted_iota(jnp.int32, (tq, tq), 0)
    c = lax.broadcasted_iota(jnp.int32, (tq, tq), 1)
    causal = c <= r
    for i in range(q_ref.shape[0] // tq):
        lo = i * tq
        qs = _split_maps(q_ref[lo:lo + tq, :])
        probs, sums = [], []
        for qm in qs:
            sd = jnp.where(causal, lax.dot_general(qm, kb[lo:lo + tq], _NT, preferred_element_type=F32), NEG_INF)
            mx = jnp.max(sd, axis=-1, keepdims=True)
            if i:
                so = lax.dot_general(qm, kb[:lo], _NT, preferred_element_type=F32)
                mx = jnp.maximum(mx, jnp.max(so, axis=-1, keepdims=True))
                po = jnp.exp(so - mx)
            pd = jnp.exp(sd - mx)
            tot = jnp.sum(pd, axis=-1, keepdims=True)
            if i:
                tot = tot + jnp.sum(po, axis=-1, keepdims=True)
            probs.append((po if i else None, pd))
            sums.append(tot)
        pv = jnp.dot(jnp.concatenate([probs[0][1], probs[1][1]], axis=0).astype(BF16), vb[lo:lo + tq],
                     preferred_element_type=F32)
        if i:
            pv = pv + jnp.dot(jnp.concatenate([probs[0][0], probs[1][0]], axis=0).astype(BF16), vb[:lo],
                              preferred_element_type=F32)
        att = pv[:tq] / sums[0] - lam * (pv[tq:] / sums[1])
        o_ref[lo:lo + tq, :] = (_rms(att, sg_ref[...]) * (1.0 - lam_init)).astype(o_ref.dtype)


def _pattn(q, k, v, lams, sg, nb, s, tq, lam_init):
    t, d = k.shape
    vec = pl.BlockSpec((1, HEAD_DIM), lambda b, h: (0, 0))
    spec = pl.BlockSpec((s, HEAD_W), lambda b, h: (b, h))
    return pl.pallas_call(
        functools.partial(_pattn_kernel, tq=tq, lam_init=lam_init),
        grid=(nb, N_HEADS),
        in_specs=[spec, spec, spec, vec, vec, vec, vec, pl.BlockSpec((1, HEAD_W), lambda b, h: (0, 0))],
        out_specs=spec,
        out_shape=jax.ShapeDtypeStruct((t, d), BF16),
        compiler_params=_params("parallel", "parallel"),
        name="pattn",
    )(q, k, v, *lams, sg)


def _sattn_kernel(pt_ref, q_ref, kn_ref, vn_ref, ck_ref, cv_ref, lq1, lk1, lq2, lk2, sg_ref, o_ref,
                  kbuf, vbuf, sem, *, n_pages, page, n_new, lam_init):
    s = pl.program_id(0)
    ns = pl.num_programs(0)
    prow = page * N_HEADS
    past = n_pages * page
    new_rows = n_new * N_HEADS

    def page_copies(seq, slot):
        out = []
        for j in range(n_pages):
            p = pt_ref[seq, j]
            out.append(pltpu.make_async_copy(ck_ref.at[p], kbuf.at[slot, pl.ds(j * prow, prow)], sem.at[slot, 0]))
            out.append(pltpu.make_async_copy(cv_ref.at[p], vbuf.at[slot, pl.ds(j * prow, prow)], sem.at[slot, 1]))
        return out

    @pl.when(s == 0)
    def _():
        for c in page_copies(0, 0):
            c.start()
        pad = kbuf.shape[1] - past * N_HEADS
        z = jnp.zeros((2, pad, HEAD_W), F32)
        kbuf[:, pl.ds(past * N_HEADS, pad), :] = z
        vbuf[:, pl.ds(past * N_HEADS, pad), :] = z

    slot = s % 2

    @pl.when(s + 1 < ns)
    def _():
        for c in page_copies(s + 1, 1 - slot):
            c.start()

    kbuf[slot, pl.ds(past * N_HEADS, new_rows), :] = kn_ref[...]
    vbuf[slot, pl.ds(past * N_HEADS, new_rows), :] = vn_ref[...]
    for c in page_copies(s, slot):
        c.wait()

    lam = _lam(lq1, lk1, lq2, lk2, lam_init)
    nq = n_new * N_HEADS
    q1, q2 = _split_maps(q_ref[...])
    qx = jnp.concatenate([q1, q2], axis=0)
    r = lax.broadcasted_iota(jnp.int32, (2 * nq, LANES), 0)
    c = lax.broadcasted_iota(jnp.int32, (2 * nq, LANES), 1)
    same_head = (c % N_HEADS) == (r % N_HEADS)
    bias_past = jnp.where(same_head, 0.0, NEG_INF)
    bias_new = jnp.where(same_head & (c // N_HEADS <= (r % nq) // N_HEADS), 0.0, NEG_INF)
    rows = kbuf.shape[1]
    chunk = 16 * LANES
    starts = list(range(0, rows, chunk))
    pieces = []
    for c0 in starts:
        w = min(chunk, rows - c0)
        kc = kbuf[slot, pl.ds(c0, w), :].astype(BF16)
        sc = lax.dot_general(qx, kc, _NT, preferred_element_type=F32)
        n_past = max(0, min(w, past * N_HEADS - c0)) // LANES
        bias = [bias_past] * n_past + [bias_new] * (w // LANES - n_past)
        pieces.append(sc + jnp.concatenate(bias, axis=1))
    sc = jnp.concatenate(pieces, axis=1)
    pr = jnp.exp(sc - jnp.max(sc, axis=-1, keepdims=True))
    tot = jnp.sum(pr, axis=-1, keepdims=True)
    pb = pr.astype(BF16)
    acc = jnp.zeros((2 * nq, HEAD_W), F32)
    for c0 in starts:
        w = min(chunk, rows - c0)
        vc = vbuf[slot, pl.ds(c0, w), :].astype(BF16)
        acc = acc + jnp.dot(pb[:, c0:c0 + w], vc, preferred_element_type=F32)
    out = acc / tot
    att = out[:nq] - lam * out[nq:]
    o_ref[...] = (_rms(att, sg_ref[...]) * (1.0 - lam_init)).astype(o_ref.dtype)


def _sattn(page_table, q, kn, vn, cache_k, cache_v, lams, sg, lam_init):
    ns, new_rows, _ = q.shape
    n_new = new_rows // N_HEADS
    n_pages = page_table.shape[1]
    page = cache_k.shape[1] // N_HEADS
    assert new_rows <= LANES
    rows = n_pages * page * N_HEADS + LANES
    vec = pl.BlockSpec((1, HEAD_DIM), lambda s, pt: (0, 0))
    newspec = pl.BlockSpec((None, new_rows, HEAD_W), lambda s, pt: (s, 0, 0))
    anyspec = pl.BlockSpec(memory_space=pl.ANY)
    grid_spec = pltpu.PrefetchScalarGridSpec(
        num_scalar_prefetch=1,
        grid=(ns,),
        in_specs=[newspec, newspec, newspec, anyspec, anyspec, vec, vec, vec, vec,
                  pl.BlockSpec((1, HEAD_W), lambda s, pt: (0, 0))],
        out_specs=newspec,
        scratch_shapes=[pltpu.VMEM((2, rows, HEAD_W), F32), pltpu.VMEM((2, rows, HEAD_W), F32),
                        pltpu.SemaphoreType.DMA((2, 2))],
    )
    return pl.pallas_call(
        functools.partial(_sattn_kernel, n_pages=n_pages, page=page, n_new=n_new, lam_init=lam_init),
        grid_spec=grid_spec,
        out_shape=jax.ShapeDtypeStruct((ns, new_rows, HEAD_W), BF16),
        compiler_params=_params("arbitrary"),
        name="sattn",
    )(page_table, q, kn, vn, cache_k, cache_v, *lams, sg)


def _s5prep_kernel(are_ref, aim_ref, ls_ref, bre_ref, bim_ref, lbr_ref, lbi_ref, bbr_ref, bbi_ref):
    a_re, a_im = are_ref[...], aim_ref[...]
    dt = jnp.exp(ls_ref[...])
    mag = jnp.exp(a_re * dt)
    ang = a_im * dt
    lb_re, lb_im = mag * jnp.cos(ang), mag * jnp.sin(ang)
    num_re, num_im = lb_re - 1.0, lb_im
    den = a_re * a_re + a_im * a_im
    f_re = (num_re * a_re + num_im * a_im) / den
    f_im = (num_im * a_re - num_re * a_im) / den
    b_re, b_im = bre_ref[...], bim_ref[...]
    lbr_ref[...] = lb_re
    lbi_ref[...] = lb_im
    bbr_ref[...] = f_re * b_re - f_im * b_im
    bbi_ref[...] = f_re * b_im + f_im * b_re


def _s5prep(a_re, a_im, log_step, b_re, b_im):
    n = a_re.shape[1]
    c = b_re.shape[0]
    row = jax.ShapeDtypeStruct((1, n), F32)
    mat = jax.ShapeDtypeStruct((c, n), F32)
    return pl.pallas_call(_s5prep_kernel, out_shape=[row, row, mat, mat], name="s5prep")(
        a_re, a_im, log_step, b_re, b_im)


def _s5_kernel(u_ref, h0r_ref, h0i_ref, lbr_ref, lbi_ref, bb_ref, cc_ref, dsk_ref, wg_ref, bg_ref,
               o_ref, hr_ref, hi_ref, st_re, st_im, bu, hb, y, *, tc, nb):
    step = pl.program_id(0)
    gw = S5_GB * S5_STATE

    @pl.when(step == 0)
    def _():
        st_re[...] = h0r_ref[...]
        st_im[...] = h0i_ref[...]

    u = u_ref[...]
    ub = u.astype(BF16)
    for gb in range(N_GB):
        bu[...] = jnp.dot(ub[:, gb * LANES:(gb + 1) * LANES], bb_ref[gb], preferred_element_type=F32)
        cols = pl.ds(gb * gw, gw)
        lr = jnp.broadcast_to(lbr_ref[:, cols], (SUBLANES, gw))
        li = jnp.broadcast_to(lbi_ref[:, cols], (SUBLANES, gw))
        for bt in range(nb // SUBLANES):
            rows = pl.ds(bt * SUBLANES, SUBLANES)

            def tstep(t, carry):
                hr, hi = carry
                r0 = pl.multiple_of(t * nb + bt * SUBLANES, SUBLANES)
                bur = bu[pl.ds(r0, SUBLANES), 0:gw]
                bui = bu[pl.ds(r0, SUBLANES), gw:2 * gw]
                nr = lr * hr - li * hi + bur
                ni = lr * hi + li * hr + bui
                hb[pl.ds(r0, SUBLANES), 0:gw] = nr
                hb[pl.ds(r0, SUBLANES), gw:2 * gw] = ni
                return nr, ni

            hr, hi = lax.fori_loop(0, tc, tstep, (st_re[rows, cols], st_im[rows, cols]), unroll=min(tc, 8))
            st_re[rows, cols] = hr
            st_im[rows, cols] = hi
        y[:, gb * LANES:(gb + 1) * LANES] = jnp.dot(hb[...].astype(BF16), cc_ref[gb],
                                                      preferred_element_type=F32)
    yy = y[...] + dsk_ref[...] * u
    z = _gelu(yy)
    gate = jnp.dot(z.astype(BF16), wg_ref[...], preferred_element_type=F32) + bg_ref[...]
    o_ref[...] = (z * jax.nn.sigmoid(gate)).astype(o_ref.dtype)

    @pl.when(step == pl.num_programs(0) - 1)
    def _():
        hr_ref[...] = st_re[...]
        hi_ref[...] = st_im[...]


def _s5(u_tm, h0_re, h0_im, lb_re, lb_im, bblk, cblk, dskip, w_glu, b_glu, nb, tc):
    rows, d = u_tm.shape
    n_state = h0_re.shape[1]
    m = tc * nb
    fix2 = lambda i: (0, 0)
    fix3 = lambda i: (0, 0, 0)
    blk = pl.BlockSpec((m, d), lambda i: (i, 0))
    st = pl.BlockSpec((nb, n_state), fix2)
    rowp = pl.BlockSpec((1, n_state), fix2)
    vecd = pl.BlockSpec((1, d), fix2)
    return pl.pallas_call(
        functools.partial(_s5_kernel, tc=tc, nb=nb),
        grid=(rows // m,),
        in_specs=[blk, st, st, rowp, rowp, pl.BlockSpec(bblk.shape, fix3), pl.BlockSpec(cblk.shape, fix3),
                  vecd, pl.BlockSpec(w_glu.shape, fix2), vecd],
        out_specs=[blk, st, st],
        out_shape=[jax.ShapeDtypeStruct((rows, d), BF16), jax.ShapeDtypeStruct((nb, n_state), F32),
                   jax.ShapeDtypeStruct((nb, n_state), F32)],
        scratch_shapes=[pltpu.VMEM((nb, n_state), F32), pltpu.VMEM((nb, n_state), F32),
                        pltpu.VMEM((m, 2 * S5_GB * S5_STATE), F32), pltpu.VMEM((m, 2 * S5_GB * S5_STATE), F32),
                        pltpu.VMEM((m, d), F32)],
        compiler_params=_params("arbitrary"),
        name="s5",
    )(u_tm, h0_re, h0_im, lb_re, lb_im, bblk, cblk, dskip, w_glu, b_glu)


def _merge_kernel(x_ref, att_ref, s5_ref, ga_ref, gs_ref, wa_ref, ws_ref, wo_ref, gf_ref, wq_ref, sk_ref,
                  x1_ref, xn_ref, sc_ref):
    pa = jnp.dot(att_ref[...], wa_ref[...], preferred_element_type=F32)
    ps = jnp.dot(s5_ref[...], ws_ref[...], preferred_element_type=F32)
    merged = jax.nn.sigmoid(ga_ref[...]) * pa + jax.nn.sigmoid(gs_ref[...]) * ps
    x1 = x_ref[...] + jnp.dot(merged.astype(BF16), wo_ref[...], preferred_element_type=F32)
    x1_ref[...] = x1
    xn = _rms(x1, gf_ref[...]).astype(BF16)
    xn_ref[...] = xn
    qt = lax.dot_general(wq_ref[...], xn, _NT, preferred_element_type=F32).astype(BF16)
    for i in range(2 * PEER_HEADS):
        rows = slice(i * PEER_HALF, (i + 1) * PEER_HALF)
        sc_ref[i * PEER_KEYS:(i + 1) * PEER_KEYS, :] = jnp.dot(sk_ref[i], qt[rows, :], preferred_element_type=F32)


def _merge(x, att, s5o, g_att, g_s5, w_att, w_s5, w_out, g_ffn, wq_t, sk, tm):
    t, d = x.shape
    row = pl.BlockSpec((tm, d), lambda i: (i, 0))
    fix2 = lambda i: (0, 0)
    wspec = pl.BlockSpec((d, d), fix2)
    n_sc = 2 * PEER_HEADS * PEER_KEYS
    return pl.pallas_call(
        _merge_kernel,
        grid=(t // tm,),
        in_specs=[row, row, row, row, row, wspec, wspec, wspec, pl.BlockSpec((1, d), fix2),
                  pl.BlockSpec(wq_t.shape, fix2), pl.BlockSpec(sk.shape, lambda i: (0, 0, 0))],
        out_specs=[row, row, pl.BlockSpec((n_sc, tm), lambda i: (0, i))],
        out_shape=[jax.ShapeDtypeStruct((t, d), F32), jax.ShapeDtypeStruct((t, d), BF16),
                   jax.ShapeDtypeStruct((n_sc, t), F32)],
        compiler_params=_params("parallel"),
        name="merge",
    )(x, att, s5o, g_att, g_s5, w_att, w_s5, w_out, g_ffn, wq_t, sk)


def _top_values(s, want_rank):
    rows = lax.broadcasted_iota(jnp.int32, (PEER_TOPK, s.shape[1]), 0)
    vals = jnp.full((PEER_TOPK, s.shape[1]), NEG_INF, F32)
    rank = jnp.full(s.shape, NO_RANK, F32)
    work = s
    for a in range(PEER_TOPK):
        m = jnp.max(work, axis=0, keepdims=True)
        hit = work == m
        vals = jnp.where(rows == a, m, vals)
        if want_rank:
            rank = jnp.where(hit, float(a), rank)
        work = jnp.where(hit, NEG_INF, work)
    return vals, rank


def _route_kernel(sc_ref, a_ref, c_ref, r2_ref, e2_ref):
    s1 = sc_ref[0:PEER_KEYS, :]
    s2 = sc_ref[PEER_KEYS:2 * PEER_KEYS, :]
    v1, _ = _top_values(s1, False)
    v2, rank2 = _top_values(s2, True)
    assert PEER_TOPK == 16
    v2lo = v2[0:SUBLANES, :]
    sub = lax.broadcasted_iota(jnp.int32, v2lo.shape, 0)
    low = sub < SUBLANES // 2
    v2x = jnp.where(low, v2lo, pltpu.roll(v2lo, SUBLANES // 2, 0))
    cand = [v1[0:1, :] + v2]
    cand += [v1[a:a + 1, :] + v2lo for a in (1, 2, 3)]
    cand += [jnp.where(low, v1[a:a + 1, :], v1[a + 1:a + 2, :]) + v2x for a in (4, 6)]
    cand.append(v1[SUBLANES:, :] + v2[0:1, :])
    cand = jnp.concatenate(cand, axis=0)
    theta = None
    for _ in range(PEER_TOPK):
        theta = jnp.max(cand, axis=0, keepdims=True)
        cand = jnp.where(cand == theta, NEG_INF, cand)
    m1, m2 = v1[0:1, :], v2[0:1, :]
    e1 = jnp.exp(s1 - m1)
    e2v = jnp.exp(v2 - m2)
    cnt = jnp.zeros_like(s1)
    g2 = jnp.zeros_like(s1)
    for b in range(SUBLANES):
        sel = (s1 + v2[b:b + 1, :]) >= theta
        cnt = cnt + jnp.where(sel, 1.0, 0.0)
        g2 = g2 + jnp.where(sel, e2v[b:b + 1, :], 0.0)
    sel_hi = (m1 + v2[SUBLANES:, :]) >= theta
    top = s1 == m1
    cnt = cnt + jnp.where(top, jnp.sum(jnp.where(sel_hi, 1.0, 0.0), axis=0, keepdims=True), 0.0)
    g2 = g2 + jnp.where(top, jnp.sum(jnp.where(sel_hi, e2v[SUBLANES:, :], 0.0), axis=0, keepdims=True), 0.0)
    z = jnp.sum(e1 * g2, axis=0, keepdims=True)
    a_ref[...] = e1 * (0.5 / z)
    c_ref[...] = cnt
    r2_ref[...] = rank2.astype(r2_ref.dtype)
    e2_ref[...] = jnp.exp(s2 - m2).astype(e2_ref.dtype)


def _route(sc, tb):
    n_sc, t = sc.shape
    n_out = PEER_HEADS * PEER_KEYS
    ospec = pl.BlockSpec((PEER_KEYS, tb), lambda i, h: (h, i))
    return pl.pallas_call(
        _route_kernel,
        grid=(t // tb, PEER_HEADS),
        in_specs=[pl.BlockSpec((2 * PEER_KEYS, tb), lambda i, h: (h, i))],
        out_specs=[ospec] * 4,
        out_shape=[jax.ShapeDtypeStruct((n_out, t), dt) for dt in (F32, F32, BF16, BF16)],
        compiler_params=_params("parallel", "parallel"),
        name="route",
    )(sc)


def _experts_kernel(xn_ref, x1_ref, u_ref, vt_ref, a_ref, c_ref, r2_ref, e2_ref, gf_ref, y_ref, acc, xnt, act_even,
                    act_odd):
    e = pl.program_id(1)
    n_e = pl.num_programs(1) - 1
    eb, tb = u_ref.shape[0], xn_ref.shape[0]
    n_i = eb // PEER_KEYS
    per = 2
    pk = 2 * SUBLANES

    @pl.when(e == 0)
    def _():
        acc[...] = jnp.zeros_like(acc)
        act_odd[...] = jnp.zeros_like(act_odd)
        xnt[...] = xn_ref[...].astype(F32).T.astype(BF16)

    def step(act_new, act_old):
        act_new[...] = jnp.dot(u_ref[...], xnt[...], preferred_element_type=F32)
        i_base = jnp.maximum(e - 1, 0) * n_i
        tiles = (PEER_KEYS // pk, pk, tb)
        zero = jnp.zeros(tiles, BF16)
        out = None
        for j in range(n_i // per):
            ws = []
            for il in range(j * per, (j + 1) * per):
                g = zero
                for h in range(PEER_HEADS):
                    hk = slice(h * PEER_KEYS, (h + 1) * PEER_KEYS)
                    row = pl.ds(h * PEER_KEYS + i_base + il, 1)
                    c_t = jnp.broadcast_to(c_ref[row, :], (pk, tb)).astype(BF16)[None]
                    a_t = jnp.broadcast_to(a_ref[row, :], (pk, tb)).astype(BF16)[None]
                    g = g + jnp.where(r2_ref[hk, :].reshape(tiles) < c_t, e2_ref[hk, :].reshape(tiles), zero) * a_t
                x = act_old[il * PEER_KEYS:(il + 1) * PEER_KEYS, :]
                ws.append(g.reshape(PEER_KEYS, tb) * (x * (1.0 + lax.erf(x * math.sqrt(0.5)))).astype(BF16))
            cols = slice(j * per * PEER_KEYS, (j + 1) * per * PEER_KEYS)
            part = jnp.dot(vt_ref[:, cols], jnp.concatenate(ws, axis=0), preferred_element_type=F32)
            out = part if out is None else out + part
        acc[...] += out

    @pl.when(e % 2 == 0)
    def _():
        step(act_even, act_odd)

    @pl.when(e % 2 == 1)
    def _():
        step(act_odd, act_even)

    @pl.when(e == n_e)
    def _():
        y_ref[...] = _rms(x1_ref[...] + acc[...].T, gf_ref[...])


def _experts(xn, x1, u, vt, a, c, r2, e2, g_final, tb, eb):
    t, d = x1.shape
    n_e = u.shape[0] // eb
    n_r = a.shape[0]
    tok = pl.BlockSpec((tb, d), lambda i, e: (i, 0))
    rt = pl.BlockSpec((n_r, tb), lambda i, e: (0, i))
    return pl.pallas_call(
        _experts_kernel,
        grid=(t // tb, n_e + 1),
        in_specs=[tok, tok, pl.BlockSpec((eb, d), lambda i, e: (jnp.minimum(e, n_e - 1), 0)),
                  pl.BlockSpec((d, eb), lambda i, e: (0, jnp.maximum(e - 1, 0))),
                  rt, rt, rt, rt, pl.BlockSpec((1, d), lambda i, e: (0, 0))],
        out_specs=tok,
        out_shape=jax.ShapeDtypeStruct((t, d), F32),
        scratch_shapes=[pltpu.VMEM((d, tb), F32), pltpu.VMEM((d, tb), BF16), pltpu.VMEM((eb, tb), F32),
                        pltpu.VMEM((eb, tb), F32)],
        compiler_params=_params("parallel", "arbitrary"),
        name="experts",
    )(xn, x1, u, vt, a, c, r2, e2, g_final)


def _block_diag_b(bb_re, bb_im):
    c = bb_re.shape[0]
    eye = jnp.eye(S5_GB, dtype=F32)

    def one(bb):
        x = bb.reshape(c, N_GB, S5_GB, S5_STATE)
        x = jnp.transpose(x, (1, 2, 0, 3))
        x = x[:, :, :, None, :] * eye[None, :, None, :, None]
        return x.reshape(N_GB, S5_GB * c, S5_GB * S5_STATE)

    return jnp.concatenate([one(bb_re), one(bb_im)], axis=-1).astype(BF16)


def _block_diag_c(c_re, c_im):
    c = c_re.shape[1]
    eye = jnp.eye(S5_GB, dtype=F32)

    def one(cc):
        x = cc.reshape(N_GB, S5_GB, c, S5_STATE)
        x = jnp.transpose(x, (0, 1, 3, 2))
        x = x[:, :, :, None, :] * eye[None, :, None, :, None]
        return x.reshape(N_GB, S5_GB * S5_STATE, S5_GB * c)

    return jnp.concatenate([one(c_re), one(-c_im)], axis=1).astype(BF16)


def _tile(t, pref):
    return pref if t % pref == 0 else t


def kernel(x_prompt, x_sample, cache_k, cache_v, state_s5_re, state_s5_im, page_table, norm_mix_g, w_in,
           lambda_q1, lambda_k1, lambda_q2, lambda_k2, subln_g, s5_a_re, s5_a_im, s5_log_step, s5_b_re, s5_b_im,
           s5_c_re, s5_c_im, s5_d, w_glu, b_glu, w_proj_att, w_proj_s5, w_out, norm_ffn_g, peer_w_query,
           peer_sub_keys, peer_u, peer_v, norm_final_g):
    depth = w_in.shape[0]
    nbp, seq, d = x_prompt.shape
    nbs, n_new, _ = x_sample.shape
    n_state = S5_GROUPS * S5_STATE
    xp = x_prompt.reshape(nbp * seq, d)
    xs = x_sample.reshape(nbs * n_new, d)
    g_final = norm_final_g.reshape(1, d)
    outs = [[] for _ in range(8)]
    for l in range(depth):
        lam_init = 0.8 - 0.6 * math.exp(-0.3 * l)
        lams = [v[l].reshape(1, HEAD_DIM) for v in (lambda_q1, lambda_k1, lambda_q2, lambda_k2)]
        sg = subln_g[l].reshape(1, HEAD_W)
        w_in_b = w_in[l].astype(BF16)
        g_mix = norm_mix_g[l].reshape(1, d)
        step_row = jnp.repeat(s5_log_step[l], S5_STATE).reshape(1, n_state)
        b_t = lambda b: jnp.transpose(b.reshape(n_state, S5_CH))
        lb_re, lb_im, bb_re, bb_im = _s5prep(s5_a_re[l].reshape(1, n_state), s5_a_im[l].reshape(1, n_state),
                                             step_row, b_t(s5_b_re[l]), b_t(s5_b_im[l]))
        bblk = _block_diag_b(bb_re, bb_im)
        cblk = _block_diag_c(s5_c_re[l], s5_c_im[l])
        dskip = s5_d[l].reshape(1, d)
        w_glu_b = w_glu[l].astype(BF16)
        bg = b_glu[l].reshape(1, d)
        w_att_b, w_s5_b, w_out_b = (w[l].astype(BF16) for w in (w_proj_att, w_proj_s5, w_out))
        g_ffn = norm_ffn_g[l].reshape(1, d)
        wq_t = jnp.transpose(peer_w_query[l]).astype(BF16)
        sk = peer_sub_keys[l].reshape(2 * PEER_HEADS, PEER_KEYS, PEER_HALF).astype(BF16)
        u_b = peer_u[l].astype(BF16)
        vt_b = jnp.transpose(peer_v[l]).astype(BF16)
        ck = cache_k[l].reshape(cache_k.shape[1], cache_k.shape[2] * N_HEADS, HEAD_W)
        cv = cache_v[l].reshape(cache_v.shape[1], cache_v.shape[2] * N_HEADS, HEAD_W)

        def mixer_tail(x, att, u, g_att, g_s5, h0_re, h0_im, nb, length, tc):
            u_tm = jnp.transpose(u.reshape(nb, length, d), (1, 0, 2)).reshape(length * nb, d)
            s5o_tm, hr, hi = _s5(u_tm, h0_re, h0_im, lb_re, lb_im, bblk, cblk, dskip, w_glu_b, bg, nb, tc)
            s5o = jnp.transpose(s5o_tm.reshape(length, nb, d), (1, 0, 2)).reshape(nb * length, d)
            t = x.shape[0]
            x1, xn, sc = _merge(x, att, s5o, g_att, g_s5, w_att_b, w_s5_b, w_out_b, g_ffn, wq_t, sk, _tile(t, 256))
            a, c, r2, e2 = _route(sc, _tile(t, 512))
            y = _experts(xn, x1, u_b, vt_b, a, c, r2, e2, g_final, _tile(t, 512), 1024)
            return y, hr, hi

        q, k, v, u, g_att, g_s5 = _inproj(xp, g_mix, w_in_b, _tile(xp.shape[0], 256))
        att = _pattn(q, k, v, lams, sg, nbp, seq, _tile(seq, 512), lam_init)
        zeros = jnp.zeros((nbp, n_state), F32)
        xp, hr_p, hi_p = mixer_tail(xp, att, u, g_att, g_s5, zeros, zeros, nbp, seq, _tile(seq, 64))
        qs, ks, vs, us, gs_att, gs_s5 = _inproj(xs, g_mix, w_in_b, _tile(xs.shape[0], 256))
        new_shp = (nbs, n_new * N_HEADS, HEAD_W)
        att_s = _sattn(page_table, qs.reshape(new_shp), ks.reshape(new_shp), vs.reshape(new_shp), ck, cv, lams, sg,
                       lam_init).reshape(nbs * n_new, d)
        xs, hr_s, hi_s = mixer_tail(xs, att_s, us, gs_att, gs_s5, state_s5_re[l].reshape(nbs, n_state),
                                    state_s5_im[l].reshape(nbs, n_state), nbs, n_new, n_new)
        kv_p = (nbp, seq, N_HEADS, HEAD_W)
        kv_s = (nbs, n_new, N_HEADS, HEAD_W)
        st = (S5_GROUPS, S5_STATE)
        for lst, val in zip(outs, (k.reshape(kv_p), v.reshape(kv_p), hr_p.reshape(nbp, *st), hi_p.reshape(nbp, *st),
                                   ks.reshape(kv_s), vs.reshape(kv_s), hr_s.reshape(nbs, *st),
                                   hi_s.reshape(nbs, *st))):
            lst.append(val)
    assert depth == 1
    return (xp.reshape(nbp, seq, d), xs.reshape(nbs, n_new, d)) + tuple(jnp.stack(o) for o in outs)
```

```python
import functools
import math

import jax
import jax.numpy as jnp
from jax import lax
from jax.experimental import pallas as pl
from jax.experimental.pallas import tpu as pltpu

F32 = jnp.float32
BF16 = jnp.bfloat16

LANES = 128
SUBLANES = 8
VMEM_LIMIT = 56 * 1024 * 1024

RMS_EPS = 1e-6
N_HEADS = 8
HEAD_DIM = 64
HEAD_W = 2 * HEAD_DIM
S5_GROUPS = 64
S5_CH = 16
S5_STATE = 64
S5_GB = 8
N_GB = S5_GROUPS // S5_GB
PEER_HEADS = 8
PEER_KEYS = 128
PEER_HALF = 128
PEER_TOPK = 16
NEG_INF = float("-inf")
NO_RANK = 99.0

_NT = (((1,), (1,)), ((), ()))


def _params(*sem):
    return pltpu.CompilerParams(dimension_semantics=sem, vmem_limit_bytes=VMEM_LIMIT)


def _rms(x, g):
    return x * lax.rsqrt(jnp.mean(x * x, axis=-1, keepdims=True) + RMS_EPS) * g


def _gelu(x):
    return 0.5 * x * (1.0 + lax.erf(x * math.sqrt(0.5)))


def _lam(lq1, lk1, lq2, lk2, lam_init):
    a = jnp.sum(lq1[...] * lk1[...], axis=-1, keepdims=True)
    b = jnp.sum(lq2[...] * lk2[...], axis=-1, keepdims=True)
    return jnp.exp(a) - jnp.exp(b) + lam_init


def _inproj_kernel(x_ref, g_ref, w_ref, q_ref, k_ref, v_ref, u_ref, ga_ref, gs_ref, *, d, q_scale):
    h = _rms(x_ref[...], g_ref[...]).astype(BF16)
    outs = (q_ref, k_ref, v_ref, u_ref, ga_ref, gs_ref)
    for j, o in enumerate(outs):
        r = jnp.dot(h, w_ref[:, j * d:(j + 1) * d], preferred_element_type=F32)
        if j == 0:
            r = r * q_scale
        o[...] = r.astype(o.dtype)


def _inproj(x, g, w, tm):
    t, d = x.shape
    row = lambda i: (i, 0)
    fix = lambda i: (0, 0)
    spec = pl.BlockSpec((tm, d), row)
    return pl.pallas_call(
        functools.partial(_inproj_kernel, d=d, q_scale=HEAD_DIM ** -0.5),
        grid=(t // tm,),
        in_specs=[spec, pl.BlockSpec((1, d), fix), pl.BlockSpec(w.shape, fix)],
        out_specs=[spec] * 6,
        out_shape=[jax.ShapeDtypeStruct((t, d), BF16)] + [jax.ShapeDtypeStruct((t, d), F32)] * 5,
        compiler_params=_params("parallel"),
        name="inproj",
    )(x, g, w)


def _split_maps(q):
    lane = lax.broadcasted_iota(jnp.int32, q.shape, 1)
    zero = jnp.zeros_like(q)
    return jnp.where(lane < HEAD_DIM, q, zero), jnp.where(lane >= HEAD_DIM, q, zero)


def _pattn_kernel(q_ref, k_ref, v_ref, lq1, lk1, lq2, lk2, sg_ref, o_ref, *, tq, lam_init):
    lam = _lam(lq1, lk1, lq2, lk2, lam_init)
    kb = k_ref[...].astype(BF16)
    vb = v_ref[...].astype(BF16)
    r = lax.broadcasted_iota(jnp.int32, (tq, tq), 0)
    c = lax.broadcasted_iota(jnp.int32, (tq, tq), 1)
    causal = c <= r
    for i in range(q_ref.shape[0] // tq):
        lo = i * tq
        qs = _split_maps(q_ref[lo:lo + tq, :])
        probs, sums = [], []
        for qm in qs:
            sd = jnp.where(causal, lax.dot_general(qm, kb[lo:lo + tq], _NT, preferred_element_type=F32), NEG_INF)
            mx = jnp.max(sd, axis=-1, keepdims=True)
            if i:
                so = lax.dot_general(qm, kb[:lo], _NT, preferred_element_type=F32)
                mx = jnp.maximum(mx, jnp.max(so, axis=-1, keepdims=True))
                po = jnp.exp(so - mx)
            pd = jnp.exp(sd - mx)
            tot = jnp.sum(pd, axis=-1, keepdims=True)
            if i:
                tot = tot + jnp.sum(po, axis=-1, keepdims=True)
            probs.append((po if i else None, pd))
            sums.append(tot)
        pv = jnp.dot(jnp.concatenate([probs[0][1], probs[1][1]], axis=0).astype(BF16), vb[lo:lo + tq],
                     preferred_element_type=F32)
        if i:
            pv = pv + jnp.dot(jnp.concatenate([probs[0][0], probs[1][0]], axis=0).astype(BF16), vb[:lo],
                              preferred_element_type=F32)
        att = pv[:tq] / sums[0] - lam * (pv[tq:] / sums[1])
        o_ref[lo:lo + tq, :] = (_rms(att, sg_ref[...]) * (1.0 - lam_init)).astype(o_ref.dtype)


def _pattn(q, k, v, lams, sg, nb, s, tq, lam_init):
    t, d = k.shape
    vec = pl.BlockSpec((1, HEAD_DIM), lambda b, h: (0, 0))
    spec = pl.BlockSpec((s, HEAD_W), lambda b, h: (b, h))
    return pl.pallas_call(
        functools.partial(_pattn_kernel, tq=tq, lam_init=lam_init),
        grid=(nb, N_HEADS),
        in_specs=[spec, spec, spec, vec, vec, vec, vec, pl.BlockSpec((1, HEAD_W), lambda b, h: (0, 0))],
        out_specs=spec,
        out_shape=jax.ShapeDtypeStruct((t, d), BF16),
        compiler_params=_params("parallel", "parallel"),
        name="pattn",
    )(q, k, v, *lams, sg)


def _sattn_kernel(pt_ref, q_ref, kn_ref, vn_ref, ck_ref, cv_ref, lq1, lk1, lq2, lk2, sg_ref, o_ref,
                  kbuf, vbuf, sem, *, n_pages, page, n_new, lam_init):
    s = pl.program_id(0)
    ns = pl.num_programs(0)
    prow = page * N_HEADS
    past = n_pages * page
    new_rows = n_new * N_HEADS

    def page_copies(seq, slot):
        out = []
        for j in range(n_pages):
            p = pt_ref[seq, j]
            out.append(pltpu.make_async_copy(ck_ref.at[p], kbuf.at[slot, pl.ds(j * prow, prow)], sem.at[slot, 0]))
            out.append(pltpu.make_async_copy(cv_ref.at[p], vbuf.at[slot, pl.ds(j * prow, prow)], sem.at[slot, 1]))
        return out

    @pl.when(s == 0)
    def _():
        for c in page_copies(0, 0):
            c.start()
        pad = kbuf.shape[1] - past * N_HEADS
        z = jnp.zeros((2, pad, HEAD_W), F32)
        kbuf[:, pl.ds(past * N_HEADS, pad), :] = z
        vbuf[:, pl.ds(past * N_HEADS, pad), :] = z

    slot = s % 2

    @pl.when(s + 1 < ns)
    def _():
        for c in page_copies(s + 1, 1 - slot):
            c.start()

    kbuf[slot, pl.ds(past * N_HEADS, new_rows), :] = kn_ref[...]
    vbuf[slot, pl.ds(past * N_HEADS, new_rows), :] = vn_ref[...]
    for c in page_copies(s, slot):
        c.wait()

    lam = _lam(lq1, lk1, lq2, lk2, lam_init)
    nq = n_new * N_HEADS
    q1, q2 = _split_maps(q_ref[...])
    qx = jnp.concatenate([q1, q2], axis=0)
    r = lax.broadcasted_iota(jnp.int32, (2 * nq, LANES), 0)
    c = lax.broadcasted_iota(jnp.int32, (2 * nq, LANES), 1)
    same_head = (c % N_HEADS) == (r % N_HEADS)
    bias_past = jnp.where(same_head, 0.0, NEG_INF)
    bias_new = jnp.where(same_head & (c // N_HEADS <= (r % nq) // N_HEADS), 0.0, NEG_INF)
    rows = kbuf.shape[1]
    chunk = 16 * LANES
    starts = list(range(0, rows, chunk))
    pieces = []
    for c0 in starts:
        w = min(chunk, rows - c0)
        kc = kbuf[slot, pl.ds(c0, w), :].astype(BF16)
        sc = lax.dot_general(qx, kc, _NT, preferred_element_type=F32)
        n_past = max(0, min(w, past * N_HEADS - c0)) // LANES
        bias = [bias_past] * n_past + [bias_new] * (w // LANES - n_past)
        pieces.append(sc + jnp.concatenate(bias, axis=1))
    sc = jnp.concatenate(pieces, axis=1)
    pr = jnp.exp(sc - jnp.max(sc, axis=-1, keepdims=True))
    tot = jnp.sum(pr, axis=-1, keepdims=True)
    pb = pr.astype(BF16)
    acc = jnp.zeros((2 * nq, HEAD_W), F32)
    for c0 in starts:
        w = min(chunk, rows - c0)
        vc = vbuf[slot, pl.ds(c0, w), :].astype(BF16)
        acc = acc + jnp.dot(pb[:, c0:c0 + w], vc, preferred_element_type=F32)
    out = acc / tot
    att = out[:nq] - lam * out[nq:]
    o_ref[...] = (_rms(att, sg_ref[...]) * (1.0 - lam_init)).astype(o_ref.dtype)


def _sattn(page_table, q, kn, vn, cache_k, cache_v, lams, sg, lam_init):
    ns, new_rows, _ = q.shape
    n_new = new_rows // N_HEADS
    n_pages = page_table.shape[1]
    page = cache_k.shape[1] // N_HEADS
    assert new_rows <= LANES
    rows = n_pages * page * N_HEADS + LANES
    vec = pl.BlockSpec((1, HEAD_DIM), lambda s, pt: (0, 0))
    newspec = pl.BlockSpec((None, new_rows, HEAD_W), lambda s, pt: (s, 0, 0))
    anyspec = pl.BlockSpec(memory_space=pl.ANY)
    grid_spec = pltpu.PrefetchScalarGridSpec(
        num_scalar_prefetch=1,
        grid=(ns,),
        in_specs=[newspec, newspec, newspec, anyspec, anyspec, vec, vec, vec, vec,
                  pl.BlockSpec((1, HEAD_W), lambda s, pt: (0, 0))],
        out_specs=newspec,
        scratch_shapes=[pltpu.VMEM((2, rows, HEAD_W), F32), pltpu.VMEM((2, rows, HEAD_W), F32),
                        pltpu.SemaphoreType.DMA((2, 2))],
    )
    return pl.pallas_call(
        functools.partial(_sattn_kernel, n_pages=n_pages, page=page, n_new=n_new, lam_init=lam_init),
        grid_spec=grid_spec,
        out_shape=jax.ShapeDtypeStruct((ns, new_rows, HEAD_W), BF16),
        compiler_params=_params("arbitrary"),
        name="sattn",
    )(page_table, q, kn, vn, cache_k, cache_v, *lams, sg)


def _s5prep_kernel(are_ref, aim_ref, ls_ref, bre_ref, bim_ref, lbr_ref, lbi_ref, bbr_ref, bbi_ref):
    a_re, a_im = are_ref[...], aim_ref[...]
    dt = jnp.exp(ls_ref[...])
    mag = jnp.exp(a_re * dt)
    ang = a_im * dt
    lb_re, lb_im = mag * jnp.cos(ang), mag * jnp.sin(ang)
    num_re, num_im = lb_re - 1.0, lb_im
    den = a_re * a_re + a_im * a_im
    f_re = (num_re * a_re + num_im * a_im) / den
    f_im = (num_im * a_re - num_re * a_im) / den
    b_re, b_im = bre_ref[...], bim_ref[...]
    lbr_ref[...] = lb_re
    lbi_ref[...] = lb_im
    bbr_ref[...] = f_re * b_re - f_im * b_im
    bbi_ref[...] = f_re * b_im + f_im * b_re


def _s5prep(a_re, a_im, log_step, b_re, b_im):
    n = a_re.shape[1]
    c = b_re.shape[0]
    row = jax.ShapeDtypeStruct((1, n), F32)
    mat = jax.ShapeDtypeStruct((c, n), F32)
    return pl.pallas_call(_s5prep_kernel, out_shape=[row, row, mat, mat], name="s5prep")(
        a_re, a_im, log_step, b_re, b_im)


def _s5_kernel(u_ref, h0r_ref, h0i_ref, lbr_ref, lbi_ref, bb_ref, cc_ref, dsk_ref, wg_ref, bg_ref,
               o_ref, hr_ref, hi_ref, st_re, st_im, bu, hb, y, *, tc, nb):
    step = pl.program_id(0)
    gw = S5_GB * S5_STATE

    @pl.when(step == 0)
    def _():
        st_re[...] = h0r_ref[...]
        st_im[...] = h0i_ref[...]

    u = u_ref[...]
    ub = u.astype(BF16)
    for gb in range(N_GB):
        bu[...] = jnp.dot(ub[:, gb * LANES:(gb + 1) * LANES], bb_ref[gb], preferred_element_type=F32)
        cols = pl.ds(gb * gw, gw)
        lr = jnp.broadcast_to(lbr_ref[:, cols], (SUBLANES, gw))
        li = jnp.broadcast_to(lbi_ref[:, cols], (SUBLANES, gw))
        for bt in range(nb // SUBLANES):
            rows = pl.ds(bt * SUBLANES, SUBLANES)

            def tstep(t, carry):
                hr, hi = carry
                r0 = pl.multiple_of(t * nb + bt * SUBLANES, SUBLANES)
                bur = bu[pl.ds(r0, SUBLANES), 0:gw]
                bui = bu[pl.ds(r0, SUBLANES), gw:2 * gw]
                nr = lr * hr - li * hi + bur
                ni = lr * hi + li * hr + bui
                hb[pl.ds(r0, SUBLANES), 0:gw] = nr
                hb[pl.ds(r0, SUBLANES), gw:2 * gw] = ni
                return nr, ni

            hr, hi = lax.fori_loop(0, tc, tstep, (st_re[rows, cols], st_im[rows, cols]), unroll=min(tc, 8))
            st_re[rows, cols] = hr
            st_im[rows, cols] = hi
        y[:, gb * LANES:(gb + 1) * LANES] = jnp.dot(hb[...].astype(BF16), cc_ref[gb],
                                                      preferred_element_type=F32)
    yy = y[...] + dsk_ref[...] * u
    z = _gelu(yy)
    gate = jnp.dot(z.astype(BF16), wg_ref[...], preferred_element_type=F32) + bg_ref[...]
    o_ref[...] = (z * jax.nn.sigmoid(gate)).astype(o_ref.dtype)

    @pl.when(step == pl.num_programs(0) - 1)
    def _():
        hr_ref[...] = st_re[...]
        hi_ref[...] = st_im[...]


def _s5(u_tm, h0_re, h0_im, lb_re, lb_im, bblk, cblk, dskip, w_glu, b_glu, nb, tc):
    rows, d = u_tm.shape
    n_state = h0_re.shape[1]
    m = tc * nb
    fix2 = lambda i: (0, 0)
    fix3 = lambda i: (0, 0, 0)
    blk = pl.BlockSpec((m, d), lambda i: (i, 0))
    st = pl.BlockSpec((nb, n_state), fix2)
    rowp = pl.BlockSpec((1, n_state), fix2)
    vecd = pl.BlockSpec((1, d), fix2)
    return pl.pallas_call(
        functools.partial(_s5_kernel, tc=tc, nb=nb),
        grid=(rows // m,),
        in_specs=[blk, st, st, rowp, rowp, pl.BlockSpec(bblk.shape, fix3), pl.BlockSpec(cblk.shape, fix3),
                  vecd, pl.BlockSpec(w_glu.shape, fix2), vecd],
        out_specs=[blk, st, st],
        out_shape=[jax.ShapeDtypeStruct((rows, d), BF16), jax.ShapeDtypeStruct((nb, n_state), F32),
                   jax.ShapeDtypeStruct((nb, n_state), F32)],
        scratch_shapes=[pltpu.VMEM((nb, n_state), F32), pltpu.VMEM((nb, n_state), F32),
                        pltpu.VMEM((m, 2 * S5_GB * S5_STATE), F32), pltpu.VMEM((m, 2 * S5_GB * S5_STATE), F32),
                        pltpu.VMEM((m, d), F32)],
        compiler_params=_params("arbitrary"),
        name="s5",
    )(u_tm, h0_re, h0_im, lb_re, lb_im, bblk, cblk, dskip, w_glu, b_glu)


def _merge_kernel(x_ref, att_ref, s5_ref, ga_ref, gs_ref, wa_ref, ws_ref, wo_ref, gf_ref, wq_ref, sk_ref,
                  x1_ref, xn_ref, sc_ref):
    pa = jnp.dot(att_ref[...], wa_ref[...], preferred_element_type=F32)
    ps = jnp.dot(s5_ref[...], ws_ref[...], preferred_element_type=F32)
    merged = jax.nn.sigmoid(ga_ref[...]) * pa + jax.nn.sigmoid(gs_ref[...]) * ps
    x1 = x_ref[...] + jnp.dot(merged.astype(BF16), wo_ref[...], preferred_element_type=F32)
    x1_ref[...] = x1
    xn = _rms(x1, gf_ref[...]).astype(BF16)
    xn_ref[...] = xn
    qt = lax.dot_general(wq_ref[...], xn, _NT, preferred_element_type=F32).astype(BF16)
    for i in range(2 * PEER_HEADS):
        rows = slice(i * PEER_HALF, (i + 1) * PEER_HALF)
        sc_ref[i * PEER_KEYS:(i + 1) * PEER_KEYS, :] = jnp.dot(sk_ref[i], qt[rows, :], preferred_element_type=F32)


def _merge(x, att, s5o, g_att, g_s5, w_att, w_s5, w_out, g_ffn, wq_t, sk, tm):
    t, d = x.shape
    row = pl.BlockSpec((tm, d), lambda i: (i, 0))
    fix2 = lambda i: (0, 0)
    wspec = pl.BlockSpec((d, d), fix2)
    n_sc = 2 * PEER_HEADS * PEER_KEYS
    return pl.pallas_call(
        _merge_kernel,
        grid=(t // tm,),
        in_specs=[row, row, row, row, row, wspec, wspec, wspec, pl.BlockSpec((1, d), fix2),
                  pl.BlockSpec(wq_t.shape, fix2), pl.BlockSpec(sk.shape, lambda i: (0, 0, 0))],
        out_specs=[row, row, pl.BlockSpec((n_sc, tm), lambda i: (0, i))],
        out_shape=[jax.ShapeDtypeStruct((t, d), F32), jax.ShapeDtypeStruct((t, d), BF16),
                   jax.ShapeDtypeStruct((n_sc, t), F32)],
        compiler_params=_params("parallel"),
        name="merge",
    )(x, att, s5o, g_att, g_s5, w_att, w_s5, w_out, g_ffn, wq_t, sk)


def _top_values(s, want_rank):
    rows = lax.broadcasted_iota(jnp.int32, (PEER_TOPK, s.shape[1]), 0)
    vals = jnp.full((PEER_TOPK, s.shape[1]), NEG_INF, F32)
    rank = jnp.full(s.shape, NO_RANK, F32)
    work = s
    for a in range(PEER_TOPK):
        m = jnp.max(work, axis=0, keepdims=True)
        hit = work == m
        vals = jnp.where(rows == a, m, vals)
        if want_rank:
            rank = jnp.where(hit, float(a), rank)
        work = jnp.where(hit, NEG_INF, work)
    return vals, rank


def _top_ranked(s):
    rows = lax.broadcasted_iota(jnp.int32, (PEER_TOPK, s.shape[1]), 0)
    idx = lax.broadcasted_iota(jnp.int32, s.shape, 0)
    vals = jnp.full((PEER_TOPK, s.shape[1]), NEG_INF, F32)
    rank = jnp.full(s.shape, NO_RANK, F32)
    live = jnp.full(s.shape, 1.0, F32)
    for a in range(PEER_TOPK):
        m = jnp.max(jnp.where(live > 0.0, s, NEG_INF), axis=0, keepdims=True)
        hit = (s == m) & (live > 0.0)
        first = jnp.min(jnp.where(hit, idx, s.shape[0]), axis=0, keepdims=True)
        pick = idx == first
        vals = jnp.where(rows == a, m, vals)
        rank = jnp.where(pick, float(a), rank)
        live = jnp.where(pick, 0.0, live)
    return vals, rank


def _count_ge(x, t):
    return jnp.sum(jnp.where(x >= t, 1.0, 0.0), axis=0, keepdims=True)


def _route_kernel(sc_ref, a_ref, c_ref, r2_ref, e2_ref):
    s1 = sc_ref[0:PEER_KEYS, :]
    s2 = sc_ref[PEER_KEYS:2 * PEER_KEYS, :]
    v1, _ = _top_values(s1, False)
    v2, rank2 = _top_values(s2, True)
    assert PEER_TOPK == 16
    v2lo = v2[0:SUBLANES, :]
    sub = lax.broadcasted_iota(jnp.int32, v2lo.shape, 0)
    low = sub < SUBLANES // 2
    v2x = jnp.where(low, v2lo, pltpu.roll(v2lo, SUBLANES // 2, 0))
    cand = [v1[0:1, :] + v2]
    cand += [v1[a:a + 1, :] + v2lo for a in (1, 2, 3)]
    cand += [jnp.where(low, v1[a:a + 1, :], v1[a + 1:a + 2, :]) + v2x for a in (4, 6)]
    cand.append(v1[SUBLANES:, :] + v2[0:1, :])
    cand = jnp.concatenate(cand, axis=0)
    work = cand
    theta = None
    for _ in range(PEER_TOPK):
        theta = jnp.max(work, axis=0, keepdims=True)
        work = jnp.where(work == theta, NEG_INF, work)
    m1, m2 = v1[0:1, :], v2[0:1, :]
    e1 = jnp.exp(s1 - m1)
    e2 = jnp.exp(s2 - m2)
    e2v = jnp.exp(v2 - m2)
    cnt = jnp.zeros_like(s1)
    g2 = jnp.zeros_like(s1)
    for b in range(SUBLANES):
        sel = (s1 + v2[b:b + 1, :]) >= theta
        cnt = cnt + jnp.where(sel, 1.0, 0.0)
        g2 = g2 + jnp.where(sel, e2v[b:b + 1, :], 0.0)
    sel_hi = (m1 + v2[SUBLANES:, :]) >= theta
    top = s1 == m1
    cnt = cnt + jnp.where(top, jnp.sum(jnp.where(sel_hi, 1.0, 0.0), axis=0, keepdims=True), 0.0)
    g2 = g2 + jnp.where(top, jnp.sum(jnp.where(sel_hi, e2v[SUBLANES:, :], 0.0), axis=0, keepdims=True), 0.0)
    z = jnp.sum(e1 * g2, axis=0, keepdims=True)
    a_ref[...] = e1 * (0.5 / z)
    c_ref[...] = cnt
    r2_ref[...] = rank2.astype(r2_ref.dtype)
    e2_ref[...] = e2.astype(e2_ref.dtype)

    k = float(PEER_TOPK)
    untied = ((_count_ge(s1, v1[PEER_TOPK - 1:, :]) == k) & (_count_ge(s2, v2[PEER_TOPK - 1:, :]) == k)
              & (_count_ge(cand, theta) == k))
    n_tied = jnp.sum(jnp.where(untied, 0.0, 1.0))

    @pl.when(n_tied > 0.0)
    def _():
        w1, rank1 = _top_ranked(s1)
        w2, rk2 = _top_ranked(s2)
        grid = jnp.concatenate([w1[a:a + 1, :] + w2 for a in range(PEER_TOPK)], axis=0)
        _, grank = _top_ranked(grid)
        chosen = grank < k
        ew1 = jnp.exp(w1 - w1[0:1, :])
        ew2 = jnp.exp(w2 - w2[0:1, :])
        cnt_t = jnp.zeros_like(s1)
        zt = jnp.zeros_like(w1[0:1, :])
        for a in range(PEER_TOPK):
            ch = chosen[a * PEER_TOPK:(a + 1) * PEER_TOPK, :]
            n_a = jnp.sum(jnp.where(ch, 1.0, 0.0), axis=0, keepdims=True)
            zt = zt + ew1[a:a + 1, :] * jnp.sum(jnp.where(ch, ew2, 0.0), axis=0, keepdims=True)
            cnt_t = jnp.where(rank1 == float(a), n_a, cnt_t)
        a_ref[...] = jnp.exp(s1 - w1[0:1, :]) * (0.5 / zt)
        c_ref[...] = cnt_t
        r2_ref[...] = rk2.astype(r2_ref.dtype)
        e2_ref[...] = jnp.exp(s2 - w2[0:1, :]).astype(e2_ref.dtype)


def _route(sc, tb):
    n_sc, t = sc.shape
    n_out = PEER_HEADS * PEER_KEYS
    ospec = pl.BlockSpec((PEER_KEYS, tb), lambda i, h: (h, i))
    return pl.pallas_call(
        _route_kernel,
        grid=(t // tb, PEER_HEADS),
        in_specs=[pl.BlockSpec((2 * PEER_KEYS, tb), lambda i, h: (h, i))],
        out_specs=[ospec] * 4,
        out_shape=[jax.ShapeDtypeStruct((n_out, t), dt) for dt in (F32, F32, BF16, BF16)],
        compiler_params=_params("parallel", "parallel"),
        name="route",
    )(sc)


def _experts_kernel(xn_ref, x1_ref, u_ref, vt_ref, a_ref, c_ref, r2_ref, e2_ref, gf_ref, y_ref, acc, xnt, act_even,
                    act_odd):
    e = pl.program_id(1)
    n_e = pl.num_programs(1) - 1
    eb, tb = u_ref.shape[0], xn_ref.shape[0]
    n_i = eb // PEER_KEYS
    per = 2
    pk = 2 * SUBLANES

    @pl.when(e == 0)
    def _():
        acc[...] = jnp.zeros_like(acc)
        act_odd[...] = jnp.zeros_like(act_odd)
        xnt[...] = xn_ref[...].astype(F32).T.astype(BF16)

    def step(act_new, act_old):
        act_new[...] = jnp.dot(u_ref[...], xnt[...], preferred_element_type=F32)
        i_base = jnp.maximum(e - 1, 0) * n_i
        tiles = (PEER_KEYS // pk, pk, tb)
        zero = jnp.zeros(tiles, BF16)
        out = None
        for j in range(n_i // per):
            ws = []
            for il in range(j * per, (j + 1) * per):
                g = zero
                for h in range(PEER_HEADS):
                    hk = slice(h * PEER_KEYS, (h + 1) * PEER_KEYS)
                    row = pl.ds(h * PEER_KEYS + i_base + il, 1)
                    c_t = jnp.broadcast_to(c_ref[row, :], (pk, tb)).astype(BF16)[None]
                    a_t = jnp.broadcast_to(a_ref[row, :], (pk, tb)).astype(BF16)[None]
                    g = g + jnp.where(r2_ref[hk, :].reshape(tiles) < c_t, e2_ref[hk, :].reshape(tiles), zero) * a_t
                x = act_old[il * PEER_KEYS:(il + 1) * PEER_KEYS, :]
                ws.append(g.reshape(PEER_KEYS, tb) * (x * (1.0 + lax.erf(x * math.sqrt(0.5)))).astype(BF16))
            cols = slice(j * per * PEER_KEYS, (j + 1) * per * PEER_KEYS)
            part = jnp.dot(vt_ref[:, cols], jnp.concatenate(ws, axis=0), preferred_element_type=F32)
            out = part if out is None else out + part
        acc[...] += out

    @pl.when(e % 2 == 0)
    def _():
        step(act_even, act_odd)

    @pl.when(e % 2 == 1)
    def _():
        step(act_odd, act_even)

    @pl.when(e == n_e)
    def _():
        y_ref[...] = _rms(x1_ref[...] + acc[...].T, gf_ref[...])


def _experts(xn, x1, u, vt, a, c, r2, e2, g_final, tb, eb):
    t, d = x1.shape
    n_e = u.shape[0] // eb
    n_r = a.shape[0]
    tok = pl.BlockSpec((tb, d), lambda i, e: (i, 0))
    rt = pl.BlockSpec((n_r, tb), lambda i, e: (0, i))
    return pl.pallas_call(
        _experts_kernel,
        grid=(t // tb, n_e + 1),
        in_specs=[tok, tok, pl.BlockSpec((eb, d), lambda i, e: (jnp.minimum(e, n_e - 1), 0)),
                  pl.BlockSpec((d, eb), lambda i, e: (0, jnp.maximum(e - 1, 0))),
                  rt, rt, rt, rt, pl.BlockSpec((1, d), lambda i, e: (0, 0))],
        out_specs=tok,
        out_shape=jax.ShapeDtypeStruct((t, d), F32),
        scratch_shapes=[pltpu.VMEM((d, tb), F32), pltpu.VMEM((d, tb), BF16), pltpu.VMEM((eb, tb), F32),
                        pltpu.VMEM((eb, tb), F32)],
        compiler_params=_params("parallel", "arbitrary"),
        name="experts",
    )(xn, x1, u, vt, a, c, r2, e2, g_final)


def _block_diag_b(bb_re, bb_im):
    c = bb_re.shape[0]
    eye = jnp.eye(S5_GB, dtype=F32)

    def one(bb):
        x = bb.reshape(c, N_GB, S5_GB, S5_STATE)
        x = jnp.transpose(x, (1, 2, 0, 3))
        x = x[:, :, :, None, :] * eye[None, :, None, :, None]
        return x.reshape(N_GB, S5_GB * c, S5_GB * S5_STATE)

    return jnp.concatenate([one(bb_re), one(bb_im)], axis=-1).astype(BF16)


def _block_diag_c(c_re, c_im):
    c = c_re.shape[1]
    eye = jnp.eye(S5_GB, dtype=F32)

    def one(cc):
        x = cc.reshape(N_GB, S5_GB, c, S5_STATE)
        x = jnp.transpose(x, (0, 1, 3, 2))
        x = x[:, :, :, None, :] * eye[None, :, None, :, None]
        return x.reshape(N_GB, S5_GB * S5_STATE, S5_GB * c)

    return jnp.concatenate([one(c_re), one(-c_im)], axis=1).astype(BF16)


ROW_TILE = 256
ATTN_Q_TILE = 512
S5_TIME_TILE = 64
PEER_TOKEN_TILE = 512
PEER_EXPERT_TILE = 1024


def _tile(t, pref):
    return pref if t % pref == 0 else t


def kernel(x_prompt, x_sample, cache_k, cache_v, state_s5_re, state_s5_im, page_table, norm_mix_g, w_in,
           lambda_q1, lambda_k1, lambda_q2, lambda_k2, subln_g, s5_a_re, s5_a_im, s5_log_step, s5_b_re, s5_b_im,
           s5_c_re, s5_c_im, s5_d, w_glu, b_glu, w_proj_att, w_proj_s5, w_out, norm_ffn_g, peer_w_query,
           peer_sub_keys, peer_u, peer_v, norm_final_g):
    depth = w_in.shape[0]
    nbp, seq, d = x_prompt.shape
    nbs, n_new, _ = x_sample.shape
    n_state = S5_GROUPS * S5_STATE
    xp = x_prompt.reshape(nbp * seq, d)
    xs = x_sample.reshape(nbs * n_new, d)
    g_final = norm_final_g.reshape(1, d)
    outs = [[] for _ in range(8)]
    for l in range(depth):
        lam_init = 0.8 - 0.6 * math.exp(-0.3 * l)
        lams = [v[l].reshape(1, HEAD_DIM) for v in (lambda_q1, lambda_k1, lambda_q2, lambda_k2)]
        sg = subln_g[l].reshape(1, HEAD_W)
        w_in_b = w_in[l].astype(BF16)
        g_mix = norm_mix_g[l].reshape(1, d)
        step_row = jnp.repeat(s5_log_step[l], S5_STATE).reshape(1, n_state)
        b_t = lambda b: jnp.transpose(b.reshape(n_state, S5_CH))
        lb_re, lb_im, bb_re, bb_im = _s5prep(s5_a_re[l].reshape(1, n_state), s5_a_im[l].reshape(1, n_state),
                                             step_row, b_t(s5_b_re[l]), b_t(s5_b_im[l]))
        bblk = _block_diag_b(bb_re, bb_im)
        cblk = _block_diag_c(s5_c_re[l], s5_c_im[l])
        dskip = s5_d[l].reshape(1, d)
        w_glu_b = w_glu[l].astype(BF16)
        bg = b_glu[l].reshape(1, d)
        w_att_b, w_s5_b, w_out_b = (w[l].astype(BF16) for w in (w_proj_att, w_proj_s5, w_out))
        g_ffn = norm_ffn_g[l].reshape(1, d)
        wq_t = jnp.transpose(peer_w_query[l]).astype(BF16)
        sk = peer_sub_keys[l].reshape(2 * PEER_HEADS, PEER_KEYS, PEER_HALF).astype(BF16)
        u_b = peer_u[l].astype(BF16)
        vt_b = jnp.transpose(peer_v[l]).astype(BF16)
        ck = cache_k[l].reshape(cache_k.shape[1], cache_k.shape[2] * N_HEADS, HEAD_W)
        cv = cache_v[l].reshape(cache_v.shape[1], cache_v.shape[2] * N_HEADS, HEAD_W)

        def mixer_tail(x, att, u, g_att, g_s5, h0_re, h0_im, nb, length, tc):
            u_tm = jnp.transpose(u.reshape(nb, length, d), (1, 0, 2)).reshape(length * nb, d)
            s5o_tm, hr, hi = _s5(u_tm, h0_re, h0_im, lb_re, lb_im, bblk, cblk, dskip, w_glu_b, bg, nb, tc)
            s5o = jnp.transpose(s5o_tm.reshape(length, nb, d), (1, 0, 2)).reshape(nb * length, d)
            t = x.shape[0]
            x1, xn, sc = _merge(x, att, s5o, g_att, g_s5, w_att_b, w_s5_b, w_out_b, g_ffn, wq_t, sk, _tile(t, ROW_TILE))
            a, c, r2, e2 = _route(sc, _tile(t, PEER_TOKEN_TILE))
            y = _experts(xn, x1, u_b, vt_b, a, c, r2, e2, g_final, _tile(t, PEER_TOKEN_TILE), PEER_EXPERT_TILE)
            return y, hr, hi

        q, k, v, u, g_att, g_s5 = _inproj(xp, g_mix, w_in_b, _tile(xp.shape[0], ROW_TILE))
        att = _pattn(q, k, v, lams, sg, nbp, seq, _tile(seq, ATTN_Q_TILE), lam_init)
        zeros = jnp.zeros((nbp, n_state), F32)
        xp, hr_p, hi_p = mixer_tail(xp, att, u, g_att, g_s5, zeros, zeros, nbp, seq, _tile(seq, S5_TIME_TILE))
        qs, ks, vs, us, gs_att, gs_s5 = _inproj(xs, g_mix, w_in_b, _tile(xs.shape[0], ROW_TILE))
        new_shp = (nbs, n_new * N_HEADS, HEAD_W)
        att_s = _sattn(page_table, qs.reshape(new_shp), ks.reshape(new_shp), vs.reshape(new_shp), ck, cv, lams, sg,
                       lam_init).reshape(nbs * n_new, d)
        xs, hr_s, hi_s = mixer_tail(xs, att_s, us, gs_att, gs_s5, state_s5_re[l].reshape(nbs, n_state),
                                    state_s5_im[l].reshape(nbs, n_state), nbs, n_new, n_new)
        kv_p = (nbp, seq, N_HEADS, HEAD_W)
        kv_s = (nbs, n_new, N_HEADS, HEAD_W)
        st = (S5_GROUPS, S5_STATE)
        for lst, val in zip(outs, (k.reshape(kv_p), v.reshape(kv_p), hr_p.reshape(nbp, *st), hi_p.reshape(nbp, *st),
                                   ks.reshape(kv_s), vs.reshape(kv_s), hr_s.reshape(nbs, *st),
                                   hi_s.reshape(nbs, *st))):
            lst.append(val)
    assert depth == 1
    return (xp.reshape(nbp, seq, d), xs.reshape(nbs, n_new, d)) + tuple(jnp.stack(o) for o in outs)
```

```python
import functools
import math

import jax
import jax.numpy as jnp
from jax import lax
from jax.experimental import pallas as pl
from jax.experimental.pallas import tpu as pltpu

F32 = jnp.float32
BF16 = jnp.bfloat16

LANES = 128
SUBLANES = 8
VMEM_LIMIT = 56 * 1024 * 1024

RMS_EPS = 1e-6
N_HEADS = 8
HEAD_DIM = 64
HEAD_W = 2 * HEAD_DIM
S5_GROUPS = 64
S5_CH = 16
S5_STATE = 64
S5_GB = 8
N_GB = S5_GROUPS // S5_GB
PEER_HEADS = 8
PEER_KEYS = 128
PEER_HALF = 128
PEER_TOPK = 16
NEG_INF = float("-inf")
NO_RANK = 99.0

_NT = (((1,), (1,)), ((), ()))


def _params(*sem):
    return pltpu.CompilerParams(dimension_semantics=sem, vmem_limit_bytes=VMEM_LIMIT)


def _rms(x, g):
    return x * lax.rsqrt(jnp.mean(x * x, axis=-1, keepdims=True) + RMS_EPS) * g


def _gelu(x):
    return 0.5 * x * (1.0 + lax.erf(x * math.sqrt(0.5)))


def _lam(lq1, lk1, lq2, lk2, lam_init):
    a = jnp.sum(lq1[...] * lk1[...], axis=-1, keepdims=True)
    b = jnp.sum(lq2[...] * lk2[...], axis=-1, keepdims=True)
    return jnp.exp(a) - jnp.exp(b) + lam_init


def _inproj_kernel(x_ref, g_ref, w_ref, q_ref, k_ref, v_ref, u_ref, ga_ref, gs_ref, *, d, q_scale):
    h = _rms(x_ref[...], g_ref[...]).astype(BF16)
    outs = (q_ref, k_ref, v_ref, u_ref, ga_ref, gs_ref)
    for j, o in enumerate(outs):
        r = jnp.dot(h, w_ref[:, j * d:(j + 1) * d], preferred_element_type=F32)
        if j == 0:
            r = r * q_scale
        o[...] = r.astype(o.dtype)


def _inproj(x, g, w, tm):
    t, d = x.shape
    row = lambda i: (i, 0)
    fix = lambda i: (0, 0)
    spec = pl.BlockSpec((tm, d), row)
    return pl.pallas_call(
        functools.partial(_inproj_kernel, d=d, q_scale=HEAD_DIM ** -0.5),
        grid=(t // tm,),
        in_specs=[spec, pl.BlockSpec((1, d), fix), pl.BlockSpec(w.shape, fix)],
        out_specs=[spec] * 6,
        out_shape=[jax.ShapeDtypeStruct((t, d), BF16)] + [jax.ShapeDtypeStruct((t, d), F32)] * 5,
        compiler_params=_params("parallel"),
        name="inproj",
    )(x, g, w)


def _split_maps(q):
    lane = lax.broadcasted_iota(jnp.int32, q.shape, 1)
    zero = jnp.zeros_like(q)
    return jnp.where(lane < HEAD_DIM, q, zero), jnp.where(lane >= HEAD_DIM, q, zero)


def _pattn_kernel(q_ref, k_ref, v_ref, lq1, lk1, lq2, lk2, sg_ref, o_ref, *, tq, lam_init):
    lam = _lam(lq1, lk1, lq2, lk2, lam_init)
    kb = k_ref[...].astype(BF16)
    vb = v_ref[...].astype(BF16)
    r = lax.broadcasted_iota(jnp.int32, (tq, tq), 0)
    c = lax.broadcasted_iota(jnp.int32, (tq, tq), 1)
    causal = c <= r
    for i in range(q_ref.shape[0] // tq):
        lo = i * tq
        qs = _split_maps(q_ref[lo:lo + tq, :])
        probs, sums = [], []
        for qm in qs:
            sd = jnp.where(causal, lax.dot_general(qm, kb[lo:lo + tq], _NT, preferred_element_type=F32), NEG_INF)
            mx = jnp.max(sd, axis=-1, keepdims=True)
            if i:
                so = lax.dot_general(qm, kb[:lo], _NT, preferred_element_type=F32)
                mx = jnp.maximum(mx, jnp.max(so, axis=-1, keepdims=True))
                po = jnp.exp(so - mx)
            pd = jnp.exp(sd - mx)
            tot = jnp.sum(pd, axis=-1, keepdims=True)
            if i:
                tot = tot + jnp.sum(po, axis=-1, keepdims=True)
            probs.append((po if i else None, pd))
            sums.append(tot)
        pv = jnp.dot(jnp.concatenate([probs[0][1], probs[1][1]], axis=0).astype(BF16), vb[lo:lo + tq],
                     preferred_element_type=F32)
        if i:
            pv = pv + jnp.dot(jnp.concatenate([probs[0][0], probs[1][0]], axis=0).astype(BF16), vb[:lo],
                              preferred_element_type=F32)
        att = pv[:tq] / sums[0] - lam * (pv[tq:] / sums[1])
        o_ref[lo:lo + tq, :] = (_rms(att, sg_ref[...]) * (1.0 - lam_init)).astype(o_ref.dtype)


def _pattn(q, k, v, lams, sg, nb, s, tq, lam_init):
    t, d = k.shape
    vec = pl.BlockSpec((1, HEAD_DIM), lambda b, h: (0, 0))
    spec = pl.BlockSpec((s, HEAD_W), lambda b, h: (b, h))
    return pl.pallas_call(
        functools.partial(_pattn_kernel, tq=tq, lam_init=lam_init),
        grid=(nb, N_HEADS),
        in_specs=[spec, spec, spec, vec, vec, vec, vec, pl.BlockSpec((1, HEAD_W), lambda b, h: (0, 0))],
        out_specs=spec,
        out_shape=jax.ShapeDtypeStruct((t, d), BF16),
        compiler_params=_params("parallel", "parallel"),
        name="pattn",
    )(q, k, v, *lams, sg)


def _sattn_kernel(pt_ref, q_ref, kn_ref, vn_ref, ck_ref, cv_ref, lq1, lk1, lq2, lk2, sg_ref, o_ref,
                  kbuf, vbuf, sem, *, n_pages, page, n_new, lam_init):
    s = pl.program_id(0)
    ns = pl.num_programs(0)
    prow = page * N_HEADS
    past = n_pages * page
    new_rows = n_new * N_HEADS

    def page_copies(seq, slot):
        out = []
        for j in range(n_pages):
            p = pt_ref[seq, j]
            out.append(pltpu.make_async_copy(ck_ref.at[p], kbuf.at[slot, pl.ds(j * prow, prow)], sem.at[slot, 0]))
            out.append(pltpu.make_async_copy(cv_ref.at[p], vbuf.at[slot, pl.ds(j * prow, prow)], sem.at[slot, 1]))
        return out

    @pl.when(s == 0)
    def _():
        for c in page_copies(0, 0):
            c.start()
        pad = kbuf.shape[1] - past * N_HEADS
        z = jnp.zeros((2, pad, HEAD_W), F32)
        kbuf[:, pl.ds(past * N_HEADS, pad), :] = z
        vbuf[:, pl.ds(past * N_HEADS, pad), :] = z

    slot = s % 2

    @pl.when(s + 1 < ns)
    def _():
        for c in page_copies(s + 1, 1 - slot):
            c.start()

    kbuf[slot, pl.ds(past * N_HEADS, new_rows), :] = kn_ref[...]
    vbuf[slot, pl.ds(past * N_HEADS, new_rows), :] = vn_ref[...]
    for c in page_copies(s, slot):
        c.wait()

    lam = _lam(lq1, lk1, lq2, lk2, lam_init)
    nq = n_new * N_HEADS
    q1, q2 = _split_maps(q_ref[...])
    qx = jnp.concatenate([q1, q2], axis=0)
    r = lax.broadcasted_iota(jnp.int32, (2 * nq, LANES), 0)
    c = lax.broadcasted_iota(jnp.int32, (2 * nq, LANES), 1)
    same_head = (c % N_HEADS) == (r % N_HEADS)
    bias_past = jnp.where(same_head, 0.0, NEG_INF)
    bias_new = jnp.where(same_head & (c // N_HEADS <= (r % nq) // N_HEADS), 0.0, NEG_INF)
    rows = kbuf.shape[1]
    chunk = 16 * LANES
    starts = list(range(0, rows, chunk))
    pieces = []
    for c0 in starts:
        w = min(chunk, rows - c0)
        kc = kbuf[slot, pl.ds(c0, w), :].astype(BF16)
        sc = lax.dot_general(qx, kc, _NT, preferred_element_type=F32)
        n_past = max(0, min(w, past * N_HEADS - c0)) // LANES
        bias = [bias_past] * n_past + [bias_new] * (w // LANES - n_past)
        pieces.append(sc + jnp.concatenate(bias, axis=1))
    sc = jnp.concatenate(pieces, axis=1)
    pr = jnp.exp(sc - jnp.max(sc, axis=-1, keepdims=True))
    tot = jnp.sum(pr, axis=-1, keepdims=True)
    pb = pr.astype(BF16)
    acc = jnp.zeros((2 * nq, HEAD_W), F32)
    for c0 in starts:
        w = min(chunk, rows - c0)
        vc = vbuf[slot, pl.ds(c0, w), :].astype(BF16)
        acc = acc + jnp.dot(pb[:, c0:c0 + w], vc, preferred_element_type=F32)
    out = acc / tot
    att = out[:nq] - lam * out[nq:]
    o_ref[...] = (_rms(att, sg_ref[...]) * (1.0 - lam_init)).astype(o_ref.dtype)


def _sattn(page_table, q, kn, vn, cache_k, cache_v, lams, sg, lam_init):
    ns, new_rows, _ = q.shape
    n_new = new_rows // N_HEADS
    n_pages = page_table.shape[1]
    page = cache_k.shape[1] // N_HEADS
    assert new_rows <= LANES
    rows = n_pages * page * N_HEADS + LANES
    vec = pl.BlockSpec((1, HEAD_DIM), lambda s, pt: (0, 0))
    newspec = pl.BlockSpec((None, new_rows, HEAD_W), lambda s, pt: (s, 0, 0))
    anyspec = pl.BlockSpec(memory_space=pl.ANY)
    grid_spec = pltpu.PrefetchScalarGridSpec(
        num_scalar_prefetch=1,
        grid=(ns,),
        in_specs=[newspec, newspec, newspec, anyspec, anyspec, vec, vec, vec, vec,
                  pl.BlockSpec((1, HEAD_W), lambda s, pt: (0, 0))],
        out_specs=newspec,
        scratch_shapes=[pltpu.VMEM((2, rows, HEAD_W), F32), pltpu.VMEM((2, rows, HEAD_W), F32),
                        pltpu.SemaphoreType.DMA((2, 2))],
    )
    return pl.pallas_call(
        functools.partial(_sattn_kernel, n_pages=n_pages, page=page, n_new=n_new, lam_init=lam_init),
        grid_spec=grid_spec,
        out_shape=jax.ShapeDtypeStruct((ns, new_rows, HEAD_W), BF16),
        compiler_params=_params("arbitrary"),
        name="sattn",
    )(page_table, q, kn, vn, cache_k, cache_v, *lams, sg)


def _s5prep_kernel(are_ref, aim_ref, ls_ref, bre_ref, bim_ref, lbr_ref, lbi_ref, bbr_ref, bbi_ref):
    a_re, a_im = are_ref[...], aim_ref[...]
    dt = jnp.exp(ls_ref[...])
    mag = jnp.exp(a_re * dt)
    ang = a_im * dt
    lb_re, lb_im = mag * jnp.cos(ang), mag * jnp.sin(ang)
    num_re, num_im = lb_re - 1.0, lb_im
    den = a_re * a_re + a_im * a_im
    f_re = (num_re * a_re + num_im * a_im) / den
    f_im = (num_im * a_re - num_re * a_im) / den
    b_re, b_im = bre_ref[...], bim_ref[...]
    lbr_ref[...] = lb_re
    lbi_ref[...] = lb_im
    bbr_ref[...] = f_re * b_re - f_im * b_im
    bbi_ref[...] = f_re * b_im + f_im * b_re


def _s5prep(a_re, a_im, log_step, b_re, b_im):
    n = a_re.shape[1]
    c = b_re.shape[0]
    row = jax.ShapeDtypeStruct((1, n), F32)
    mat = jax.ShapeDtypeStruct((c, n), F32)
    return pl.pallas_call(_s5prep_kernel, out_shape=[row, row, mat, mat], name="s5prep")(
        a_re, a_im, log_step, b_re, b_im)


def _s5_kernel(u_ref, h0r_ref, h0i_ref, lbr_ref, lbi_ref, bb_ref, cc_ref, dsk_ref, wg_ref, bg_ref,
               o_ref, hr_ref, hi_ref, st_re, st_im, bu, hb, y, *, tc, nb):
    step = pl.program_id(0)
    gw = S5_GB * S5_STATE

    @pl.when(step == 0)
    def _():
        st_re[...] = h0r_ref[...]
        st_im[...] = h0i_ref[...]

    u = u_ref[...]
    ub = u.astype(BF16)
    for gb in range(N_GB):
        bu[...] = jnp.dot(ub[:, gb * LANES:(gb + 1) * LANES], bb_ref[gb], preferred_element_type=F32)
        cols = pl.ds(gb * gw, gw)
        lr = jnp.broadcast_to(lbr_ref[:, cols], (SUBLANES, gw))
        li = jnp.broadcast_to(lbi_ref[:, cols], (SUBLANES, gw))
        for bt in range(nb // SUBLANES):
            rows = pl.ds(bt * SUBLANES, SUBLANES)

            def tstep(t, carry):
                hr, hi = carry
                r0 = pl.multiple_of(t * nb + bt * SUBLANES, SUBLANES)
                bur = bu[pl.ds(r0, SUBLANES), 0:gw]
                bui = bu[pl.ds(r0, SUBLANES), gw:2 * gw]
                nr = lr * hr - li * hi + bur
                ni = lr * hi + li * hr + bui
                hb[pl.ds(r0, SUBLANES), 0:gw] = nr
                hb[pl.ds(r0, SUBLANES), gw:2 * gw] = ni
                return nr, ni

            hr, hi = lax.fori_loop(0, tc, tstep, (st_re[rows, cols], st_im[rows, cols]), unroll=min(tc, 8))
            st_re[rows, cols] = hr
            st_im[rows, cols] = hi
        y[:, gb * LANES:(gb + 1) * LANES] = jnp.dot(hb[...].astype(BF16), cc_ref[gb],
                                                      preferred_element_type=F32)
    yy = y[...] + dsk_ref[...] * u
    z = _gelu(yy)
    gate = jnp.dot(z.astype(BF16), wg_ref[...], preferred_element_type=F32) + bg_ref[...]
    o_ref[...] = (z * jax.nn.sigmoid(gate)).astype(o_ref.dtype)

    @pl.when(step == pl.num_programs(0) - 1)
    def _():
        hr_ref[...] = st_re[...]
        hi_ref[...] = st_im[...]


def _s5(u_tm, h0_re, h0_im, lb_re, lb_im, bblk, cblk, dskip, w_glu, b_glu, nb, tc):
    rows, d = u_tm.shape
    n_state = h0_re.shape[1]
    m = tc * nb
    fix2 = lambda i: (0, 0)
    fix3 = lambda i: (0, 0, 0)
    blk = pl.BlockSpec((m, d), lambda i: (i, 0))
    st = pl.BlockSpec((nb, n_state), fix2)
    rowp = pl.BlockSpec((1, n_state), fix2)
    vecd = pl.BlockSpec((1, d), fix2)
    return pl.pallas_call(
        functools.partial(_s5_kernel, tc=tc, nb=nb),
        grid=(rows // m,),
        in_specs=[blk, st, st, rowp, rowp, pl.BlockSpec(bblk.shape, fix3), pl.BlockSpec(cblk.shape, fix3),
                  vecd, pl.BlockSpec(w_glu.shape, fix2), vecd],
        out_specs=[blk, st, st],
        out_shape=[jax.ShapeDtypeStruct((rows, d), BF16), jax.ShapeDtypeStruct((nb, n_state), F32),
                   jax.ShapeDtypeStruct((nb, n_state), F32)],
        scratch_shapes=[pltpu.VMEM((nb, n_state), F32), pltpu.VMEM((nb, n_state), F32),
                        pltpu.VMEM((m, 2 * S5_GB * S5_STATE), F32), pltpu.VMEM((m, 2 * S5_GB * S5_STATE), F32),
                        pltpu.VMEM((m, d), F32)],
        compiler_params=_params("arbitrary"),
        name="s5",
    )(u_tm, h0_re, h0_im, lb_re, lb_im, bblk, cblk, dskip, w_glu, b_glu)


def _merge_kernel(x_ref, att_ref, s5_ref, ga_ref, gs_ref, wa_ref, ws_ref, wo_ref, gf_ref, wq_ref, sk_ref,
                  x1_ref, xn_ref, sc_ref):
    pa = jnp.dot(att_ref[...], wa_ref[...], preferred_element_type=F32)
    ps = jnp.dot(s5_ref[...], ws_ref[...], preferred_element_type=F32)
    merged = jax.nn.sigmoid(ga_ref[...]) * pa + jax.nn.sigmoid(gs_ref[...]) * ps
    x1 = x_ref[...] + jnp.dot(merged.astype(BF16), wo_ref[...], preferred_element_type=F32)
    x1_ref[...] = x1
    xn = _rms(x1, gf_ref[...]).astype(BF16)
    xn_ref[...] = xn
    qt = lax.dot_general(wq_ref[...], xn, _NT, preferred_element_type=F32).astype(BF16)
    for i in range(2 * PEER_HEADS):
        rows = slice(i * PEER_HALF, (i + 1) * PEER_HALF)
        sc_ref[i * PEER_KEYS:(i + 1) * PEER_KEYS, :] = jnp.dot(sk_ref[i], qt[rows, :], preferred_element_type=F32)


def _merge(x, att, s5o, g_att, g_s5, w_att, w_s5, w_out, g_ffn, wq_t, sk, tm):
    t, d = x.shape
    row = pl.BlockSpec((tm, d), lambda i: (i, 0))
    fix2 = lambda i: (0, 0)
    wspec = pl.BlockSpec((d, d), fix2)
    n_sc = 2 * PEER_HEADS * PEER_KEYS
    return pl.pallas_call(
        _merge_kernel,
        grid=(t // tm,),
        in_specs=[row, row, row, row, row, wspec, wspec, wspec, pl.BlockSpec((1, d), fix2),
                  pl.BlockSpec(wq_t.shape, fix2), pl.BlockSpec(sk.shape, lambda i: (0, 0, 0))],
        out_specs=[row, row, pl.BlockSpec((n_sc, tm), lambda i: (0, i))],
        out_shape=[jax.ShapeDtypeStruct((t, d), F32), jax.ShapeDtypeStruct((t, d), BF16),
                   jax.ShapeDtypeStruct((n_sc, t), F32)],
        compiler_params=_params("parallel"),
        name="merge",
    )(x, att, s5o, g_att, g_s5, w_att, w_s5, w_out, g_ffn, wq_t, sk)


def _top_values(s, want_rank):
    rows = lax.broadcasted_iota(jnp.int32, (PEER_TOPK, s.shape[1]), 0)
    vals = jnp.full((PEER_TOPK, s.shape[1]), NEG_INF, F32)
    rank = jnp.full(s.shape, NO_RANK, F32)
    work = s
    for a in range(PEER_TOPK):
        m = jnp.max(work, axis=0, keepdims=True)
        hit = work == m
        vals = jnp.where(rows == a, m, vals)
        if want_rank:
            rank = jnp.where(hit, float(a), rank)
        work = jnp.where(hit, NEG_INF, work)
    return vals, rank


def _top_ranked(s):
    rows = lax.broadcasted_iota(jnp.int32, (PEER_TOPK, s.shape[1]), 0)
    idx = lax.broadcasted_iota(jnp.int32, s.shape, 0)
    vals = jnp.full((PEER_TOPK, s.shape[1]), NEG_INF, F32)
    rank = jnp.full(s.shape, NO_RANK, F32)
    live = jnp.full(s.shape, 1.0, F32)
    for a in range(PEER_TOPK):
        m = jnp.max(jnp.where(live > 0.0, s, NEG_INF), axis=0, keepdims=True)
        hit = (s == m) & (live > 0.0)
        first = jnp.min(jnp.where(hit, idx, s.shape[0]), axis=0, keepdims=True)
        pick = idx == first
        vals = jnp.where(rows == a, m, vals)
        rank = jnp.where(pick, float(a), rank)
        live = jnp.where(pick, 0.0, live)
    return vals, rank


def _count_ge(x, t):
    return jnp.sum(jnp.where(x >= t, 1.0, 0.0), axis=0, keepdims=True)


def _route_kernel(sc_ref, a_ref, c_ref, r2_ref, e2_ref):
    s1 = sc_ref[0:PEER_KEYS, :]
    s2 = sc_ref[PEER_KEYS:2 * PEER_KEYS, :]
    v1, _ = _top_values(s1, False)
    v2, rank2 = _top_values(s2, True)
    assert PEER_TOPK == 16
    v2lo = v2[0:SUBLANES, :]
    sub = lax.broadcasted_iota(jnp.int32, v2lo.shape, 0)
    low = sub < SUBLANES // 2
    v2x = jnp.where(low, v2lo, pltpu.roll(v2lo, SUBLANES // 2, 0))
    cand = [v1[0:1, :] + v2]
    cand += [v1[a:a + 1, :] + v2lo for a in (1, 2, 3)]
    cand += [jnp.where(low, v1[a:a + 1, :], v1[a + 1:a + 2, :]) + v2x for a in (4, 6)]
    cand.append(v1[SUBLANES:, :] + v2[0:1, :])
    cand = jnp.concatenate(cand, axis=0)
    work = cand
    theta = None
    for _ in range(PEER_TOPK):
        theta = jnp.max(work, axis=0, keepdims=True)
        work = jnp.where(work == theta, NEG_INF, work)
    m1, m2 = v1[0:1, :], v2[0:1, :]
    e1 = jnp.exp(s1 - m1)
    e2 = jnp.exp(s2 - m2)
    e2v = jnp.exp(v2 - m2)
    cnt = jnp.zeros_like(s1)
    g2 = jnp.zeros_like(s1)
    for b in range(SUBLANES):
        sel = (s1 + v2[b:b + 1, :]) >= theta
        cnt = cnt + jnp.where(sel, 1.0, 0.0)
        g2 = g2 + jnp.where(sel, e2v[b:b + 1, :], 0.0)
    sel_hi = (m1 + v2[SUBLANES:, :]) >= theta
    top = s1 == m1
    cnt = cnt + jnp.where(top, jnp.sum(jnp.where(sel_hi, 1.0, 0.0), axis=0, keepdims=True), 0.0)
    g2 = g2 + jnp.where(top, jnp.sum(jnp.where(sel_hi, e2v[SUBLANES:, :], 0.0), axis=0, keepdims=True), 0.0)
    k = float(PEER_TOPK)
    clean = (jnp.sum(cnt, axis=0, keepdims=True) == k) & (_count_ge(s2, v2[PEER_TOPK - 1:, :]) == k)
    tied = jnp.where(clean, 0.0, 1.0)
    any_tied = jnp.sum(tied)

    z = jnp.sum(e1 * g2, axis=0, keepdims=True)
    a_ref[...] = e1 * (0.5 / z)
    c_ref[...] = cnt
    r2_ref[...] = rank2.astype(r2_ref.dtype)
    e2_ref[...] = e2.astype(e2_ref.dtype)
    tiles = [slice(c0, c0 + LANES) for c0 in range(0, s1.shape[1], LANES)]

    def redo(cols):
        t1, t2 = s1[:, cols], s2[:, cols]
        w1, rank1 = _top_ranked(t1)
        w2, rk2 = _top_ranked(t2)
        grid = jnp.concatenate([w1[a:a + 1, :] + w2 for a in range(PEER_TOPK)], axis=0)
        _, grank = _top_ranked(grid)
        chosen = grank < k
        ew1 = jnp.exp(w1 - w1[0:1, :])
        ew2 = jnp.exp(w2 - w2[0:1, :])
        cnt_t = jnp.zeros_like(t1)
        zt = jnp.zeros_like(w1[0:1, :])
        for a in range(PEER_TOPK):
            ch = chosen[a * PEER_TOPK:(a + 1) * PEER_TOPK, :]
            n_a = jnp.sum(jnp.where(ch, 1.0, 0.0), axis=0, keepdims=True)
            zt = zt + ew1[a:a + 1, :] * jnp.sum(jnp.where(ch, ew2, 0.0), axis=0, keepdims=True)
            cnt_t = jnp.where(rank1 == float(a), n_a, cnt_t)
        a_ref[:, cols] = jnp.exp(t1 - w1[0:1, :]) * (0.5 / zt)
        c_ref[:, cols] = cnt_t
        r2_ref[:, cols] = rk2.astype(r2_ref.dtype)
        e2_ref[:, cols] = jnp.exp(t2 - w2[0:1, :]).astype(e2_ref.dtype)

    @pl.when(any_tied > 0.0)
    def _():
        n_tied = [jnp.sum(tied[:, cols]) for cols in tiles]
        for cols, n in zip(tiles, n_tied):
            pl.when(n > 0.0)(functools.partial(redo, cols))


def _route(sc, tb):
    n_sc, t = sc.shape
    n_out = PEER_HEADS * PEER_KEYS
    ospec = pl.BlockSpec((PEER_KEYS, tb), lambda i, h: (h, i))
    return pl.pallas_call(
        _route_kernel,
        grid=(t // tb, PEER_HEADS),
        in_specs=[pl.BlockSpec((2 * PEER_KEYS, tb), lambda i, h: (h, i))],
        out_specs=[ospec] * 4,
        out_shape=[jax.ShapeDtypeStruct((n_out, t), dt) for dt in (F32, F32, BF16, BF16)],
        compiler_params=_params("parallel", "parallel"),
        name="route",
    )(sc)


def _experts_kernel(xn_ref, x1_ref, u_ref, vt_ref, a_ref, c_ref, r2_ref, e2_ref, gf_ref, y_ref, acc, xnt, act_even,
                    act_odd):
    e = pl.program_id(1)
    n_e = pl.num_programs(1) - 1
    eb, tb = u_ref.shape[0], xn_ref.shape[0]
    n_i = eb // PEER_KEYS
    per = 2
    pk = 2 * SUBLANES

    @pl.when(e == 0)
    def _():
        acc[...] = jnp.zeros_like(acc)
        act_odd[...] = jnp.zeros_like(act_odd)
        xnt[...] = xn_ref[...].astype(F32).T.astype(BF16)

    def step(act_new, act_old):
        act_new[...] = jnp.dot(u_ref[...], xnt[...], preferred_element_type=F32)
        i_base = jnp.maximum(e - 1, 0) * n_i
        tiles = (PEER_KEYS // pk, pk, tb)
        zero = jnp.zeros(tiles, BF16)
        out = None
        for j in range(n_i // per):
            ws = []
            for il in range(j * per, (j + 1) * per):
                g = zero
                for h in range(PEER_HEADS):
                    hk = slice(h * PEER_KEYS, (h + 1) * PEER_KEYS)
                    row = pl.ds(h * PEER_KEYS + i_base + il, 1)
                    c_t = jnp.broadcast_to(c_ref[row, :], (pk, tb)).astype(BF16)[None]
                    a_t = jnp.broadcast_to(a_ref[row, :], (pk, tb)).astype(BF16)[None]
                    g = g + jnp.where(r2_ref[hk, :].reshape(tiles) < c_t, e2_ref[hk, :].reshape(tiles), zero) * a_t
                x = act_old[il * PEER_KEYS:(il + 1) * PEER_KEYS, :]
                ws.append(g.reshape(PEER_KEYS, tb) * (x * (1.0 + lax.erf(x * math.sqrt(0.5)))).astype(BF16))
            cols = slice(j * per * PEER_KEYS, (j + 1) * per * PEER_KEYS)
            part = jnp.dot(vt_ref[:, cols], jnp.concatenate(ws, axis=0), preferred_element_type=F32)
            out = part if out is None else out + part
        acc[...] += out

    @pl.when(e % 2 == 0)
    def _():
        step(act_even, act_odd)

    @pl.when(e % 2 == 1)
    def _():
        step(act_odd, act_even)

    @pl.when(e == n_e)
    def _():
        y_ref[...] = _rms(x1_ref[...] + acc[...].T, gf_ref[...])


def _experts(xn, x1, u, vt, a, c, r2, e2, g_final, tb, eb):
    t, d = x1.shape
    n_e = u.shape[0] // eb
    n_r = a.shape[0]
    tok = pl.BlockSpec((tb, d), lambda i, e: (i, 0))
    rt = pl.BlockSpec((n_r, tb), lambda i, e: (0, i))
    return pl.pallas_call(
        _experts_kernel,
        grid=(t // tb, n_e + 1),
        in_specs=[tok, tok, pl.BlockSpec((eb, d), lambda i, e: (jnp.minimum(e, n_e - 1), 0)),
                  pl.BlockSpec((d, eb), lambda i, e: (0, jnp.maximum(e - 1, 0))),
                  rt, rt, rt, rt, pl.BlockSpec((1, d), lambda i, e: (0, 0))],
        out_specs=tok,
        out_shape=jax.ShapeDtypeStruct((t, d), F32),
        scratch_shapes=[pltpu.VMEM((d, tb), F32), pltpu.VMEM((d, tb), BF16), pltpu.VMEM((eb, tb), F32),
                        pltpu.VMEM((eb, tb), F32)],
        compiler_params=_params("parallel", "arbitrary"),
        name="experts",
    )(xn, x1, u, vt, a, c, r2, e2, g_final)


def _block_diag_b(bb_re, bb_im):
    c = bb_re.shape[0]
    eye = jnp.eye(S5_GB, dtype=F32)

    def one(bb):
        x = bb.reshape(c, N_GB, S5_GB, S5_STATE)
        x = jnp.transpose(x, (1, 2, 0, 3))
        x = x[:, :, :, None, :] * eye[None, :, None, :, None]
        return x.reshape(N_GB, S5_GB * c, S5_GB * S5_STATE)

    return jnp.concatenate([one(bb_re), one(bb_im)], axis=-1).astype(BF16)


def _block_diag_c(c_re, c_im):
    c = c_re.shape[1]
    eye = jnp.eye(S5_GB, dtype=F32)

    def one(cc):
        x = cc.reshape(N_GB, S5_GB, c, S5_STATE)
        x = jnp.transpose(x, (0, 1, 3, 2))
        x = x[:, :, :, None, :] * eye[None, :, None, :, None]
        return x.reshape(N_GB, S5_GB * S5_STATE, S5_GB * c)

    return jnp.concatenate([one(c_re), one(-c_im)], axis=1).astype(BF16)


ROW_TILE = 256
ATTN_Q_TILE = 512
S5_TIME_TILE = 64
PEER_TOKEN_TILE = 512
PEER_EXPERT_TILE = 1024


def _tile(t, pref):
    return pref if t % pref == 0 else t


def kernel(x_prompt, x_sample, cache_k, cache_v, state_s5_re, state_s5_im, page_table, norm_mix_g, w_in,
           lambda_q1, lambda_k1, lambda_q2, lambda_k2, subln_g, s5_a_re, s5_a_im, s5_log_step, s5_b_re, s5_b_im,
           s5_c_re, s5_c_im, s5_d, w_glu, b_glu, w_proj_att, w_proj_s5, w_out, norm_ffn_g, peer_w_query,
           peer_sub_keys, peer_u, peer_v, norm_final_g):
    depth = w_in.shape[0]
    nbp, seq, d = x_prompt.shape
    nbs, n_new, _ = x_sample.shape
    n_state = S5_GROUPS * S5_STATE
    xp = x_prompt.reshape(nbp * seq, d)
    xs = x_sample.reshape(nbs * n_new, d)
    g_final = norm_final_g.reshape(1, d)
    outs = [[] for _ in range(8)]
    for l in range(depth):
        lam_init = 0.8 - 0.6 * math.exp(-0.3 * l)
        lams = [v[l].reshape(1, HEAD_DIM) for v in (lambda_q1, lambda_k1, lambda_q2, lambda_k2)]
        sg = subln_g[l].reshape(1, HEAD_W)
        w_in_b = w_in[l].astype(BF16)
        g_mix = norm_mix_g[l].reshape(1, d)
        step_row = jnp.repeat(s5_log_step[l], S5_STATE).reshape(1, n_state)
        b_t = lambda b: jnp.transpose(b.reshape(n_state, S5_CH))
        lb_re, lb_im, bb_re, bb_im = _s5prep(s5_a_re[l].reshape(1, n_state), s5_a_im[l].reshape(1, n_state),
                                             step_row, b_t(s5_b_re[l]), b_t(s5_b_im[l]))
        bblk = _block_diag_b(bb_re, bb_im)
        cblk = _block_diag_c(s5_c_re[l], s5_c_im[l])
        dskip = s5_d[l].reshape(1, d)
        w_glu_b = w_glu[l].astype(BF16)
        bg = b_glu[l].reshape(1, d)
        w_att_b, w_s5_b, w_out_b = (w[l].astype(BF16) for w in (w_proj_att, w_proj_s5, w_out))
        g_ffn = norm_ffn_g[l].reshape(1, d)
        wq_t = jnp.transpose(peer_w_query[l]).astype(BF16)
        sk = peer_sub_keys[l].reshape(2 * PEER_HEADS, PEER_KEYS, PEER_HALF).astype(BF16)
        u_b = peer_u[l].astype(BF16)
        vt_b = jnp.transpose(peer_v[l]).astype(BF16)
        ck = cache_k[l].reshape(cache_k.shape[1], cache_k.shape[2] * N_HEADS, HEAD_W)
        cv = cache_v[l].reshape(cache_v.shape[1], cache_v.shape[2] * N_HEADS, HEAD_W)

        def mixer_tail(x, att, u, g_att, g_s5, h0_re, h0_im, nb, length, tc):
            u_tm = jnp.transpose(u.reshape(nb, length, d), (1, 0, 2)).reshape(length * nb, d)
            s5o_tm, hr, hi = _s5(u_tm, h0_re, h0_im, lb_re, lb_im, bblk, cblk, dskip, w_glu_b, bg, nb, tc)
            s5o = jnp.transpose(s5o_tm.reshape(length, nb, d), (1, 0, 2)).reshape(nb * length, d)
            t = x.shape[0]
            x1, xn, sc = _merge(x, att, s5o, g_att, g_s5, w_att_b, w_s5_b, w_out_b, g_ffn, wq_t, sk, _tile(t, ROW_TILE))
            a, c, r2, e2 = _route(sc, _tile(t, PEER_TOKEN_TILE))
            y = _experts(xn, x1, u_b, vt_b, a, c, r2, e2, g_final, _tile(t, PEER_TOKEN_TILE), PEER_EXPERT_TILE)
            return y, hr, hi

        q, k, v, u, g_att, g_s5 = _inproj(xp, g_mix, w_in_b, _tile(xp.shape[0], ROW_TILE))
        att = _pattn(q, k, v, lams, sg, nbp, seq, _tile(seq, ATTN_Q_TILE), lam_init)
        zeros = jnp.zeros((nbp, n_state), F32)
        xp, hr_p, hi_p = mixer_tail(xp, att, u, g_att, g_s5, zeros, zeros, nbp, seq, _tile(seq, S5_TIME_TILE))
        qs, ks, vs, us, gs_att, gs_s5 = _inproj(xs, g_mix, w_in_b, _tile(xs.shape[0], ROW_TILE))
        new_shp = (nbs, n_new * N_HEADS, HEAD_W)
        att_s = _sattn(page_table, qs.reshape(new_shp), ks.reshape(new_shp), vs.reshape(new_shp), ck, cv, lams, sg,
                       lam_init).reshape(nbs * n_new, d)
        xs, hr_s, hi_s = mixer_tail(xs, att_s, us, gs_att, gs_s5, state_s5_re[l].reshape(nbs, n_state),
                                    state_s5_im[l].reshape(nbs, n_state), nbs, n_new, n_new)
        kv_p = (nbp, seq, N_HEADS, HEAD_W)
        kv_s = (nbs, n_new, N_HEADS, HEAD_W)
        st = (S5_GROUPS, S5_STATE)
        for lst, val in zip(outs, (k.reshape(kv_p), v.reshape(kv_p), hr_p.reshape(nbp, *st), hi_p.reshape(nbp, *st),
                                   ks.reshape(kv_s), vs.reshape(kv_s), hr_s.reshape(nbs, *st),
                                   hi_s.reshape(nbs, *st))):
            lst.append(val)
    assert depth == 1
    return (xp.reshape(nbp, seq, d), xs.reshape(nbs, n_new, d)) + tuple(jnp.stack(o) for o in outs)
```

```python
import functools
import math

import jax
import jax.numpy as jnp
from jax import lax
from jax.experimental import pallas as pl
from jax.experimental.pallas import tpu as pltpu

F32 = jnp.float32
BF16 = jnp.bfloat16

LANES = 128
SUBLANES = 8
VMEM_LIMIT = 56 * 1024 * 1024

RMS_EPS = 1e-6
N_HEADS = 8
HEAD_DIM = 64
HEAD_W = 2 * HEAD_DIM
S5_GROUPS = 64
S5_CH = 16
S5_STATE = 64
S5_GB = 8
N_GB = S5_GROUPS // S5_GB
PEER_HEADS = 8
PEER_KEYS = 128
PEER_HALF = 128
PEER_TOPK = 16
NEG_INF = float("-inf")
NO_RANK = 99.0

_NT = (((1,), (1,)), ((), ()))


def _params(*sem):
    return pltpu.CompilerParams(dimension_semantics=sem, vmem_limit_bytes=VMEM_LIMIT)


def _rms(x, g):
    return x * lax.rsqrt(jnp.mean(x * x, axis=-1, keepdims=True) + RMS_EPS) * g


def _gelu(x):
    return 0.5 * x * (1.0 + lax.erf(x * math.sqrt(0.5)))


def _lam(lq1, lk1, lq2, lk2, lam_init):
    a = jnp.sum(lq1[...] * lk1[...], axis=-1, keepdims=True)
    b = jnp.sum(lq2[...] * lk2[...], axis=-1, keepdims=True)
    return jnp.exp(a) - jnp.exp(b) + lam_init


def _inproj_kernel(x_ref, g_ref, w_ref, q_ref, k_ref, v_ref, u_ref, ga_ref, gs_ref, *, d, q_scale):
    h = _rms(x_ref[...], g_ref[...]).astype(BF16)
    outs = (q_ref, k_ref, v_ref, u_ref, ga_ref, gs_ref)
    for j, o in enumerate(outs):
        r = jnp.dot(h, w_ref[:, j * d:(j + 1) * d], preferred_element_type=F32)
        if j == 0:
            r = r * q_scale
        o[...] = r.astype(o.dtype)


def _inproj(x, g, w, tm):
    t, d = x.shape
    row = lambda i: (i, 0)
    fix = lambda i: (0, 0)
    spec = pl.BlockSpec((tm, d), row)
    return pl.pallas_call(
        functools.partial(_inproj_kernel, d=d, q_scale=HEAD_DIM ** -0.5),
        grid=(t // tm,),
        in_specs=[spec, pl.BlockSpec((1, d), fix), pl.BlockSpec(w.shape, fix)],
        out_specs=[spec] * 6,
        out_shape=[jax.ShapeDtypeStruct((t, d), BF16)] + [jax.ShapeDtypeStruct((t, d), F32)] * 5,
        compiler_params=_params("parallel"),
        name="inproj",
    )(x, g, w)


def _split_maps(q):
    lane = lax.broadcasted_iota(jnp.int32, q.shape, 1)
    zero = jnp.zeros_like(q)
    return jnp.where(lane < HEAD_DIM, q, zero), jnp.where(lane >= HEAD_DIM, q, zero)


def _pattn_kernel(q_ref, k_ref, v_ref, lq1, lk1, lq2, lk2, sg_ref, o_ref, *, tq, lam_init):
    lam = _lam(lq1, lk1, lq2, lk2, lam_init)
    kb = k_ref[...].astype(BF16)
    vb = v_ref[...].astype(BF16)
    r = lax.broadcasted_iota(jnp.int32, (tq, tq), 0)
    c = lax.broadcasted_iota(jnp.int32, (tq, tq), 1)
    causal = c <= r
    for i in range(q_ref.shape[0] // tq):
        lo = i * tq
        qs = _split_maps(q_ref[lo:lo + tq, :])
        probs, sums = [], []
        for qm in qs:
            sd = jnp.where(causal, lax.dot_general(qm, kb[lo:lo + tq], _NT, preferred_element_type=F32), NEG_INF)
            mx = jnp.max(sd, axis=-1, keepdims=True)
            if i:
                so = lax.dot_general(qm, kb[:lo], _NT, preferred_element_type=F32)
                mx = jnp.maximum(mx, jnp.max(so, axis=-1, keepdims=True))
                po = jnp.exp(so - mx)
            pd = jnp.exp(sd - mx)
            tot = jnp.sum(pd, axis=-1, keepdims=True)
            if i:
                tot = tot + jnp.sum(po, axis=-1, keepdims=True)
            probs.append((po if i else None, pd))
            sums.append(tot)
        pv = jnp.dot(jnp.concatenate([probs[0][1], probs[1][1]], axis=0).astype(BF16), vb[lo:lo + tq],
                     preferred_element_type=F32)
        if i:
            pv = pv + jnp.dot(jnp.concatenate([probs[0][0], probs[1][0]], axis=0).astype(BF16), vb[:lo],
                              preferred_element_type=F32)
        att = pv[:tq] / sums[0] - lam * (pv[tq:] / sums[1])
        o_ref[lo:lo + tq, :] = (_rms(att, sg_ref[...]) * (1.0 - lam_init)).astype(o_ref.dtype)


def _pattn(q, k, v, lams, sg, nb, s, tq, lam_init):
    t, d = k.shape
    vec = pl.BlockSpec((1, HEAD_DIM), lambda b, h: (0, 0))
    spec = pl.BlockSpec((s, HEAD_W), lambda b, h: (b, h))
    return pl.pallas_call(
        functools.partial(_pattn_kernel, tq=tq, lam_init=lam_init),
        grid=(nb, N_HEADS),
        in_specs=[spec, spec, spec, vec, vec, vec, vec, pl.BlockSpec((1, HEAD_W), lambda b, h: (0, 0))],
        out_specs=spec,
        out_shape=jax.ShapeDtypeStruct((t, d), BF16),
        compiler_params=_params("parallel", "parallel"),
        name="pattn",
    )(q, k, v, *lams, sg)


def _sattn_kernel(pt_ref, q_ref, kn_ref, vn_ref, ck_ref, cv_ref, lq1, lk1, lq2, lk2, sg_ref, o_ref,
                  kbuf, vbuf, sem, *, n_pages, page, n_new, lam_init):
    s = pl.program_id(0)
    ns = pl.num_programs(0)
    prow = page * N_HEADS
    past = n_pages * page
    new_rows = n_new * N_HEADS

    def page_copies(seq, slot):
        out = []
        for j in range(n_pages):
            p = pt_ref[seq, j]
            out.append(pltpu.make_async_copy(ck_ref.at[p], kbuf.at[slot, pl.ds(j * prow, prow)], sem.at[slot, 0]))
            out.append(pltpu.make_async_copy(cv_ref.at[p], vbuf.at[slot, pl.ds(j * prow, prow)], sem.at[slot, 1]))
        return out

    @pl.when(s == 0)
    def _():
        for c in page_copies(0, 0):
            c.start()
        pad = kbuf.shape[1] - past * N_HEADS
        z = jnp.zeros((2, pad, HEAD_W), F32)
        kbuf[:, pl.ds(past * N_HEADS, pad), :] = z
        vbuf[:, pl.ds(past * N_HEADS, pad), :] = z

    slot = s % 2

    @pl.when(s + 1 < ns)
    def _():
        for c in page_copies(s + 1, 1 - slot):
            c.start()

    kbuf[slot, pl.ds(past * N_HEADS, new_rows), :] = kn_ref[...]
    vbuf[slot, pl.ds(past * N_HEADS, new_rows), :] = vn_ref[...]
    for c in page_copies(s, slot):
        c.wait()

    lam = _lam(lq1, lk1, lq2, lk2, lam_init)
    nq = n_new * N_HEADS
    q1, q2 = _split_maps(q_ref[...])
    qx = jnp.concatenate([q1, q2], axis=0)
    r = lax.broadcasted_iota(jnp.int32, (2 * nq, LANES), 0)
    c = lax.broadcasted_iota(jnp.int32, (2 * nq, LANES), 1)
    same_head = (c % N_HEADS) == (r % N_HEADS)
    bias_past = jnp.where(same_head, 0.0, NEG_INF)
    bias_new = jnp.where(same_head & (c // N_HEADS <= (r % nq) // N_HEADS), 0.0, NEG_INF)
    rows = kbuf.shape[1]
    chunk = 16 * LANES
    starts = list(range(0, rows, chunk))
    pieces = []
    for c0 in starts:
        w = min(chunk, rows - c0)
        kc = kbuf[slot, pl.ds(c0, w), :].astype(BF16)
        sc = lax.dot_general(qx, kc, _NT, preferred_element_type=F32)
        n_past = max(0, min(w, past * N_HEADS - c0)) // LANES
        bias = [bias_past] * n_past + [bias_new] * (w // LANES - n_past)
        pieces.append(sc + jnp.concatenate(bias, axis=1))
    sc = jnp.concatenate(pieces, axis=1)
    pr = jnp.exp(sc - jnp.max(sc, axis=-1, keepdims=True))
    tot = jnp.sum(pr, axis=-1, keepdims=True)
    pb = pr.astype(BF16)
    acc = jnp.zeros((2 * nq, HEAD_W), F32)
    for c0 in starts:
        w = min(chunk, rows - c0)
        vc = vbuf[slot, pl.ds(c0, w), :].astype(BF16)
        acc = acc + jnp.dot(pb[:, c0:c0 + w], vc, preferred_element_type=F32)
    out = acc / tot
    att = out[:nq] - lam * out[nq:]
    o_ref[...] = (_rms(att, sg_ref[...]) * (1.0 - lam_init)).astype(o_ref.dtype)


def _sattn(page_table, q, kn, vn, cache_k, cache_v, lams, sg, lam_init):
    ns, new_rows, _ = q.shape
    n_new = new_rows // N_HEADS
    n_pages = page_table.shape[1]
    page = cache_k.shape[1] // N_HEADS
    assert new_rows <= LANES
    rows = n_pages * page * N_HEADS + LANES
    vec = pl.BlockSpec((1, HEAD_DIM), lambda s, pt: (0, 0))
    newspec = pl.BlockSpec((None, new_rows, HEAD_W), lambda s, pt: (s, 0, 0))
    anyspec = pl.BlockSpec(memory_space=pl.ANY)
    grid_spec = pltpu.PrefetchScalarGridSpec(
        num_scalar_prefetch=1,
        grid=(ns,),
        in_specs=[newspec, newspec, newspec, anyspec, anyspec, vec, vec, vec, vec,
                  pl.BlockSpec((1, HEAD_W), lambda s, pt: (0, 0))],
        out_specs=newspec,
        scratch_shapes=[pltpu.VMEM((2, rows, HEAD_W), F32), pltpu.VMEM((2, rows, HEAD_W), F32),
                        pltpu.SemaphoreType.DMA((2, 2))],
    )
    return pl.pallas_call(
        functools.partial(_sattn_kernel, n_pages=n_pages, page=page, n_new=n_new, lam_init=lam_init),
        grid_spec=grid_spec,
        out_shape=jax.ShapeDtypeStruct((ns, new_rows, HEAD_W), BF16),
        compiler_params=_params("arbitrary"),
        name="sattn",
    )(page_table, q, kn, vn, cache_k, cache_v, *lams, sg)


def _s5prep_kernel(are_ref, aim_ref, ls_ref, bre_ref, bim_ref, lbr_ref, lbi_ref, bbr_ref, bbi_ref):
    a_re, a_im = are_ref[...], aim_ref[...]
    dt = jnp.exp(ls_ref[...])
    mag = jnp.exp(a_re * dt)
    ang = a_im * dt
    lb_re, lb_im = mag * jnp.cos(ang), mag * jnp.sin(ang)
    num_re, num_im = lb_re - 1.0, lb_im
    den = a_re * a_re + a_im * a_im
    f_re = (num_re * a_re + num_im * a_im) / den
    f_im = (num_im * a_re - num_re * a_im) / den
    b_re, b_im = bre_ref[...], bim_ref[...]
    lbr_ref[...] = lb_re
    lbi_ref[...] = lb_im
    bbr_ref[...] = f_re * b_re - f_im * b_im
    bbi_ref[...] = f_re * b_im + f_im * b_re


def _s5prep(a_re, a_im, log_step, b_re, b_im):
    n = a_re.shape[1]
    c = b_re.shape[0]
    row = jax.ShapeDtypeStruct((1, n), F32)
    mat = jax.ShapeDtypeStruct((c, n), F32)
    return pl.pallas_call(_s5prep_kernel, out_shape=[row, row, mat, mat], name="s5prep")(
        a_re, a_im, log_step, b_re, b_im)


def _s5_kernel(u_ref, h0r_ref, h0i_ref, lbr_ref, lbi_ref, bb_ref, cc_ref, dsk_ref, wg_ref, bg_ref,
               o_ref, hr_ref, hi_ref, st_re, st_im, bu, hb, y, *, tc, nb):
    step = pl.program_id(0)
    gw = S5_GB * S5_STATE

    @pl.when(step == 0)
    def _():
        st_re[...] = h0r_ref[...]
        st_im[...] = h0i_ref[...]

    u = u_ref[...]
    ub = u.astype(BF16)
    for gb in range(N_GB):
        bu[...] = jnp.dot(ub[:, gb * LANES:(gb + 1) * LANES], bb_ref[gb], preferred_element_type=F32)
        cols = pl.ds(gb * gw, gw)
        lr = jnp.broadcast_to(lbr_ref[:, cols], (SUBLANES, gw))
        li = jnp.broadcast_to(lbi_ref[:, cols], (SUBLANES, gw))
        for bt in range(nb // SUBLANES):
            rows = pl.ds(bt * SUBLANES, SUBLANES)

            def tstep(t, carry):
                hr, hi = carry
                r0 = pl.multiple_of(t * nb + bt * SUBLANES, SUBLANES)
                bur = bu[pl.ds(r0, SUBLANES), 0:gw]
                bui = bu[pl.ds(r0, SUBLANES), gw:2 * gw]
                nr = lr * hr - li * hi + bur
                ni = lr * hi + li * hr + bui
                hb[pl.ds(r0, SUBLANES), 0:gw] = nr
                hb[pl.ds(r0, SUBLANES), gw:2 * gw] = ni
                return nr, ni

            hr, hi = lax.fori_loop(0, tc, tstep, (st_re[rows, cols], st_im[rows, cols]), unroll=min(tc, 8))
            st_re[rows, cols] = hr
            st_im[rows, cols] = hi
        y[:, gb * LANES:(gb + 1) * LANES] = jnp.dot(hb[...].astype(BF16), cc_ref[gb],
                                                      preferred_element_type=F32)
    yy = y[...] + dsk_ref[...] * u
    z = _gelu(yy)
    gate = jnp.dot(z.astype(BF16), wg_ref[...], preferred_element_type=F32) + bg_ref[...]
    o_ref[...] = (z * jax.nn.sigmoid(gate)).astype(o_ref.dtype)

    @pl.when(step == pl.num_programs(0) - 1)
    def _():
        hr_ref[...] = st_re[...]
        hi_ref[...] = st_im[...]


def _s5(u_tm, h0_re, h0_im, lb_re, lb_im, bblk, cblk, dskip, w_glu, b_glu, nb, tc):
    rows, d = u_tm.shape
    n_state = h0_re.shape[1]
    m = tc * nb
    fix2 = lambda i: (0, 0)
    fix3 = lambda i: (0, 0, 0)
    blk = pl.BlockSpec((m, d), lambda i: (i, 0))
    st = pl.BlockSpec((nb, n_state), fix2)
    rowp = pl.BlockSpec((1, n_state), fix2)
    vecd = pl.BlockSpec((1, d), fix2)
    return pl.pallas_call(
        functools.partial(_s5_kernel, tc=tc, nb=nb),
        grid=(rows // m,),
        in_specs=[blk, st, st, rowp, rowp, pl.BlockSpec(bblk.shape, fix3), pl.BlockSpec(cblk.shape, fix3),
                  vecd, pl.BlockSpec(w_glu.shape, fix2), vecd],
        out_specs=[blk, st, st],
        out_shape=[jax.ShapeDtypeStruct((rows, d), BF16), jax.ShapeDtypeStruct((nb, n_state), F32),
                   jax.ShapeDtypeStruct((nb, n_state), F32)],
        scratch_shapes=[pltpu.VMEM((nb, n_state), F32), pltpu.VMEM((nb, n_state), F32),
                        pltpu.VMEM((m, 2 * S5_GB * S5_STATE), F32), pltpu.VMEM((m, 2 * S5_GB * S5_STATE), F32),
                        pltpu.VMEM((m, d), F32)],
        compiler_params=_params("arbitrary"),
        name="s5",
    )(u_tm, h0_re, h0_im, lb_re, lb_im, bblk, cblk, dskip, w_glu, b_glu)


def _merge_kernel(x_ref, att_ref, s5_ref, ga_ref, gs_ref, wa_ref, ws_ref, wo_ref, gf_ref, wq_ref, sk_ref,
                  x1_ref, xn_ref, sc_ref):
    pa = jnp.dot(att_ref[...], wa_ref[...], preferred_element_type=F32)
    ps = jnp.dot(s5_ref[...], ws_ref[...], preferred_element_type=F32)
    merged = jax.nn.sigmoid(ga_ref[...]) * pa + jax.nn.sigmoid(gs_ref[...]) * ps
    x1 = x_ref[...] + jnp.dot(merged.astype(BF16), wo_ref[...], preferred_element_type=F32)
    x1_ref[...] = x1
    xn = _rms(x1, gf_ref[...]).astype(BF16)
    xn_ref[...] = xn
    qt = lax.dot_general(wq_ref[...], xn, _NT, preferred_element_type=F32).astype(BF16)
    for i in range(2 * PEER_HEADS):
        rows = slice(i * PEER_HALF, (i + 1) * PEER_HALF)
        sc_ref[i * PEER_KEYS:(i + 1) * PEER_KEYS, :] = jnp.dot(sk_ref[i], qt[rows, :], preferred_element_type=F32)


def _merge(x, att, s5o, g_att, g_s5, w_att, w_s5, w_out, g_ffn, wq_t, sk, tm):
    t, d = x.shape
    row = pl.BlockSpec((tm, d), lambda i: (i, 0))
    fix2 = lambda i: (0, 0)
    wspec = pl.BlockSpec((d, d), fix2)
    n_sc = 2 * PEER_HEADS * PEER_KEYS
    return pl.pallas_call(
        _merge_kernel,
        grid=(t // tm,),
        in_specs=[row, row, row, row, row, wspec, wspec, wspec, pl.BlockSpec((1, d), fix2),
                  pl.BlockSpec(wq_t.shape, fix2), pl.BlockSpec(sk.shape, lambda i: (0, 0, 0))],
        out_specs=[row, row, pl.BlockSpec((n_sc, tm), lambda i: (0, i))],
        out_shape=[jax.ShapeDtypeStruct((t, d), F32), jax.ShapeDtypeStruct((t, d), BF16),
                   jax.ShapeDtypeStruct((n_sc, t), F32)],
        compiler_params=_params("parallel"),
        name="merge",
    )(x, att, s5o, g_att, g_s5, w_att, w_s5, w_out, g_ffn, wq_t, sk)


def _top_values(s, want_rank):
    rows = lax.broadcasted_iota(jnp.int32, (PEER_TOPK, s.shape[1]), 0)
    vals = jnp.full((PEER_TOPK, s.shape[1]), NEG_INF, F32)
    rank = jnp.full(s.shape, NO_RANK, F32)
    work = s
    for a in range(PEER_TOPK):
        m = jnp.max(work, axis=0, keepdims=True)
        hit = work == m
        vals = jnp.where(rows == a, m, vals)
        if want_rank:
            rank = jnp.where(hit, float(a), rank)
        work = jnp.where(hit, NEG_INF, work)
    return vals, rank


def _top_ranked(s):
    rows = lax.broadcasted_iota(jnp.int32, (PEER_TOPK, s.shape[1]), 0)
    idx = lax.broadcasted_iota(jnp.int32, s.shape, 0)
    vals = jnp.full((PEER_TOPK, s.shape[1]), NEG_INF, F32)
    rank = jnp.full(s.shape, NO_RANK, F32)
    live = jnp.full(s.shape, 1.0, F32)
    for a in range(PEER_TOPK):
        m = jnp.max(jnp.where(live > 0.0, s, NEG_INF), axis=0, keepdims=True)
        hit = (s == m) & (live > 0.0)
        first = jnp.min(jnp.where(hit, idx, s.shape[0]), axis=0, keepdims=True)
        pick = idx == first
        vals = jnp.where(rows == a, m, vals)
        rank = jnp.where(pick, float(a), rank)
        live = jnp.where(pick, 0.0, live)
    return vals, rank


def _count_ge(x, t):
    return jnp.sum(jnp.where(x >= t, 1.0, 0.0), axis=0, keepdims=True)


def _route_kernel(sc_ref, a_ref, c_ref, r2_ref, e2_ref):
    s1 = sc_ref[0:PEER_KEYS, :]
    s2 = sc_ref[PEER_KEYS:2 * PEER_KEYS, :]
    v1, _ = _top_values(s1, False)
    v2, rank2 = _top_values(s2, True)
    assert PEER_TOPK == 16
    v2lo = v2[0:SUBLANES, :]
    sub = lax.broadcasted_iota(jnp.int32, v2lo.shape, 0)
    low = sub < SUBLANES // 2
    v2x = jnp.where(low, v2lo, pltpu.roll(v2lo, SUBLANES // 2, 0))
    cand = [v1[0:1, :] + v2]
    cand += [v1[a:a + 1, :] + v2lo for a in (1, 2, 3)]
    cand += [jnp.where(low, v1[a:a + 1, :], v1[a + 1:a + 2, :]) + v2x for a in (4, 6)]
    cand.append(v1[SUBLANES:, :] + v2[0:1, :])
    cand = jnp.concatenate(cand, axis=0)
    work = cand
    theta = None
    for _ in range(PEER_TOPK):
        theta = jnp.max(work, axis=0, keepdims=True)
        work = jnp.where(work == theta, NEG_INF, work)
    m1, m2 = v1[0:1, :], v2[0:1, :]
    e1 = jnp.exp(s1 - m1)
    e2 = jnp.exp(s2 - m2)
    e2v = jnp.exp(v2 - m2)
    cnt = jnp.zeros_like(s1)
    g2 = jnp.zeros_like(s1)
    for b in range(SUBLANES):
        sel = (s1 + v2[b:b + 1, :]) >= theta
        cnt = cnt + jnp.where(sel, 1.0, 0.0)
        g2 = g2 + jnp.where(sel, e2v[b:b + 1, :], 0.0)
    sel_hi = (m1 + v2[SUBLANES:, :]) >= theta
    top = s1 == m1
    cnt = cnt + jnp.where(top, jnp.sum(jnp.where(sel_hi, 1.0, 0.0), axis=0, keepdims=True), 0.0)
    g2 = g2 + jnp.where(top, jnp.sum(jnp.where(sel_hi, e2v[SUBLANES:, :], 0.0), axis=0, keepdims=True), 0.0)
    k = float(PEER_TOPK)
    clean = (jnp.sum(cnt, axis=0, keepdims=True) == k) & (_count_ge(s2, v2[PEER_TOPK - 1:, :]) == k)
    tied = jnp.where(clean, 0.0, 1.0)
    any_tied = jnp.sum(tied)

    z = jnp.sum(e1 * g2, axis=0, keepdims=True)
    a_ref[...] = e1 * (0.5 / z)
    c_ref[...] = cnt
    r2_ref[...] = rank2.astype(r2_ref.dtype)
    e2_ref[...] = e2.astype(e2_ref.dtype)
    tiles = [slice(c0, c0 + LANES) for c0 in range(0, s1.shape[1], LANES)]

    def redo(cols):
        t1, t2 = s1[:, cols], s2[:, cols]
        w1, rank1 = _top_ranked(t1)
        w2, rk2 = _top_ranked(t2)
        grid = jnp.concatenate([w1[a:a + 1, :] + w2 for a in range(PEER_TOPK)], axis=0)
        _, grank = _top_ranked(grid)
        chosen = grank < k
        ew1 = jnp.exp(w1 - w1[0:1, :])
        ew2 = jnp.exp(w2 - w2[0:1, :])
        cnt_t = jnp.zeros_like(t1)
        zt = jnp.zeros_like(w1[0:1, :])
        for a in range(PEER_TOPK):
            ch = chosen[a * PEER_TOPK:(a + 1) * PEER_TOPK, :]
            n_a = jnp.sum(jnp.where(ch, 1.0, 0.0), axis=0, keepdims=True)
            zt = zt + ew1[a:a + 1, :] * jnp.sum(jnp.where(ch, ew2, 0.0), axis=0, keepdims=True)
            cnt_t = jnp.where(rank1 == float(a), n_a, cnt_t)
        a_ref[:, cols] = jnp.exp(t1 - w1[0:1, :]) * (0.5 / zt)
        c_ref[:, cols] = cnt_t
        r2_ref[:, cols] = rk2.astype(r2_ref.dtype)
        e2_ref[:, cols] = jnp.exp(t2 - w2[0:1, :]).astype(e2_ref.dtype)

    @pl.when(any_tied > 0.0)
    def _():
        n_tied = [jnp.sum(tied[:, cols]) for cols in tiles]
        for cols, n in zip(tiles, n_tied):
            pl.when(n > 0.0)(functools.partial(redo, cols))


def _route(sc, tb):
    n_sc, t = sc.shape
    n_out = PEER_HEADS * PEER_KEYS
    ospec = pl.BlockSpec((PEER_KEYS, tb), lambda i, h: (h, i))
    return pl.pallas_call(
        _route_kernel,
        grid=(t // tb, PEER_HEADS),
        in_specs=[pl.BlockSpec((2 * PEER_KEYS, tb), lambda i, h: (h, i))],
        out_specs=[ospec] * 4,
        out_shape=[jax.ShapeDtypeStruct((n_out, t), dt) for dt in (F32, F32, BF16, BF16)],
        compiler_params=_params("parallel", "parallel"),
        name="route",
    )(sc)


def _experts_kernel(xn_ref, x1_ref, u_ref, vt_ref, a_ref, c_ref, r2_ref, e2_ref, gf_ref, y_ref, acc, xnt, act_even,
                    act_odd):
    e = pl.program_id(1)
    n_e = pl.num_programs(1) - 1
    eb, tb = u_ref.shape[0], xn_ref.shape[0]
    n_i = eb // PEER_KEYS
    per = 2
    pk = 2 * SUBLANES

    @pl.when(e == 0)
    def _():
        acc[...] = jnp.zeros_like(acc)
        act_odd[...] = jnp.zeros_like(act_odd)
        xnt[...] = xn_ref[...].astype(F32).T.astype(BF16)

    def step(act_new, act_old):
        act_new[...] = jnp.dot(u_ref[...], xnt[...], preferred_element_type=F32)
        i_base = jnp.maximum(e - 1, 0) * n_i
        tiles = (PEER_KEYS // pk, pk, tb)
        zero = jnp.zeros(tiles, BF16)
        out = None
        for j in range(n_i // per):
            ws = []
            for il in range(j * per, (j + 1) * per):
                g = zero
                for h in range(PEER_HEADS):
                    hk = slice(h * PEER_KEYS, (h + 1) * PEER_KEYS)
                    row = pl.ds(h * PEER_KEYS + i_base + il, 1)
                    c_t = jnp.broadcast_to(c_ref[row, :], (pk, tb)).astype(BF16)[None]
                    a_t = jnp.broadcast_to(a_ref[row, :], (pk, tb)).astype(BF16)[None]
                    g = g + jnp.where(r2_ref[hk, :].reshape(tiles) < c_t, e2_ref[hk, :].reshape(tiles), zero) * a_t
                x = act_old[il * PEER_KEYS:(il + 1) * PEER_KEYS, :]
                ws.append(g.reshape(PEER_KEYS, tb) * (x * (1.0 + lax.erf(x * math.sqrt(0.5)))).astype(BF16))
            cols = slice(j * per * PEER_KEYS, (j + 1) * per * PEER_KEYS)
            part = jnp.dot(vt_ref[:, cols], jnp.concatenate(ws, axis=0), preferred_element_type=F32)
            out = part if out is None else out + part
        acc[...] += out

    @pl.when(e % 2 == 0)
    def _():
        step(act_even, act_odd)

    @pl.when(e % 2 == 1)
    def _():
        step(act_odd, act_even)

    @pl.when(e == n_e)
    def _():
        y_ref[...] = _rms(x1_ref[...] + acc[...].T, gf_ref[...])


def _experts(xn, x1, u, vt, a, c, r2, e2, g_final, tb, eb):
    t, d = x1.shape
    n_e = u.shape[0] // eb
    n_r = a.shape[0]
    tok = pl.BlockSpec((tb, d), lambda i, e: (i, 0))
    rt = pl.BlockSpec((n_r, tb), lambda i, e: (0, i))
    return pl.pallas_call(
        _experts_kernel,
        grid=(t // tb, n_e + 1),
        in_specs=[tok, tok, pl.BlockSpec((eb, d), lambda i, e: (jnp.minimum(e, n_e - 1), 0)),
                  pl.BlockSpec((d, eb), lambda i, e: (0, jnp.maximum(e - 1, 0))),
                  rt, rt, rt, rt, pl.BlockSpec((1, d), lambda i, e: (0, 0))],
        out_specs=tok,
        out_shape=jax.ShapeDtypeStruct((t, d), F32),
        scratch_shapes=[pltpu.VMEM((d, tb), F32), pltpu.VMEM((d, tb), BF16), pltpu.VMEM((eb, tb), F32),
                        pltpu.VMEM((eb, tb), F32)],
        compiler_params=_params("parallel", "arbitrary"),
        name="experts",
    )(xn, x1, u, vt, a, c, r2, e2, g_final)


def _block_diag_b(bb_re, bb_im):
    c = bb_re.shape[0]
    eye = jnp.eye(S5_GB, dtype=F32)

    def one(bb):
        x = bb.reshape(c, N_GB, S5_GB, S5_STATE)
        x = jnp.transpose(x, (1, 2, 0, 3))
        x = x[:, :, :, None, :] * eye[None, :, None, :, None]
        return x.reshape(N_GB, S5_GB * c, S5_GB * S5_STATE)

    return jnp.concatenate([one(bb_re), one(bb_im)], axis=-1).astype(BF16)


def _block_diag_c(c_re, c_im):
    c = c_re.shape[1]
    eye = jnp.eye(S5_GB, dtype=F32)

    def one(cc):
        x = cc.reshape(N_GB, S5_GB, c, S5_STATE)
        x = jnp.transpose(x, (0, 1, 3, 2))
        x = x[:, :, :, None, :] * eye[None, :, None, :, None]
        return x.reshape(N_GB, S5_GB * S5_STATE, S5_GB * c)

    return jnp.concatenate([one(c_re), one(-c_im)], axis=1).astype(BF16)


ROW_TILE = 256
ATTN_Q_TILE = 256
S5_TIME_TILE = 64
PEER_TOKEN_TILE = 512
PEER_EXPERT_TILE = 1024


def _tile(t, pref):
    return pref if t % pref == 0 else t


def kernel(x_prompt, x_sample, cache_k, cache_v, state_s5_re, state_s5_im, page_table, norm_mix_g, w_in,
           lambda_q1, lambda_k1, lambda_q2, lambda_k2, subln_g, s5_a_re, s5_a_im, s5_log_step, s5_b_re, s5_b_im,
           s5_c_re, s5_c_im, s5_d, w_glu, b_glu, w_proj_att, w_proj_s5, w_out, norm_ffn_g, peer_w_query,
           peer_sub_keys, peer_u, peer_v, norm_final_g):
    depth = w_in.shape[0]
    nbp, seq, d = x_prompt.shape
    nbs, n_new, _ = x_sample.shape
    n_state = S5_GROUPS * S5_STATE
    xp = x_prompt.reshape(nbp * seq, d)
    xs = x_sample.reshape(nbs * n_new, d)
    g_final = norm_final_g.reshape(1, d)
    outs = [[] for _ in range(8)]
    for l in range(depth):
        lam_init = 0.8 - 0.6 * math.exp(-0.3 * l)
        lams = [v[l].reshape(1, HEAD_DIM) for v in (lambda_q1, lambda_k1, lambda_q2, lambda_k2)]
        sg = subln_g[l].reshape(1, HEAD_W)
        w_in_b = w_in[l].astype(BF16)
        g_mix = norm_mix_g[l].reshape(1, d)
        step_row = jnp.repeat(s5_log_step[l], S5_STATE).reshape(1, n_state)
        b_t = lambda b: jnp.transpose(b.reshape(n_state, S5_CH))
        lb_re, lb_im, bb_re, bb_im = _s5prep(s5_a_re[l].reshape(1, n_state), s5_a_im[l].reshape(1, n_state),
                                             step_row, b_t(s5_b_re[l]), b_t(s5_b_im[l]))
        bblk = _block_diag_b(bb_re, bb_im)
        cblk = _block_diag_c(s5_c_re[l], s5_c_im[l])
        dskip = s5_d[l].reshape(1, d)
        w_glu_b = w_glu[l].astype(BF16)
        bg = b_glu[l].reshape(1, d)
        w_att_b, w_s5_b, w_out_b = (w[l].astype(BF16) for w in (w_proj_att, w_proj_s5, w_out))
        g_ffn = norm_ffn_g[l].reshape(1, d)
        wq_t = jnp.transpose(peer_w_query[l]).astype(BF16)
        sk = peer_sub_keys[l].reshape(2 * PEER_HEADS, PEER_KEYS, PEER_HALF).astype(BF16)
        u_b = peer_u[l].astype(BF16)
        vt_b = jnp.transpose(peer_v[l]).astype(BF16)
        ck = cache_k[l].reshape(cache_k.shape[1], cache_k.shape[2] * N_HEADS, HEAD_W)
        cv = cache_v[l].reshape(cache_v.shape[1], cache_v.shape[2] * N_HEADS, HEAD_W)

        def mixer_tail(x, att, u, g_att, g_s5, h0_re, h0_im, nb, length, tc):
            u_tm = jnp.transpose(u.reshape(nb, length, d), (1, 0, 2)).reshape(length * nb, d)
            s5o_tm, hr, hi = _s5(u_tm, h0_re, h0_im, lb_re, lb_im, bblk, cblk, dskip, w_glu_b, bg, nb, tc)
            s5o = jnp.transpose(s5o_tm.reshape(length, nb, d), (1, 0, 2)).reshape(nb * length, d)
            t = x.shape[0]
            x1, xn, sc = _merge(x, att, s5o, g_att, g_s5, w_att_b, w_s5_b, w_out_b, g_ffn, wq_t, sk, _tile(t, ROW_TILE))
            a, c, r2, e2 = _route(sc, _tile(t, PEER_TOKEN_TILE))
            y = _experts(xn, x1, u_b, vt_b, a, c, r2, e2, g_final, _tile(t, PEER_TOKEN_TILE), PEER_EXPERT_TILE)
            return y, hr, hi

        q, k, v, u, g_att, g_s5 = _inproj(xp, g_mix, w_in_b, _tile(xp.shape[0], ROW_TILE))
        att = _pattn(q, k, v, lams, sg, nbp, seq, _tile(seq, ATTN_Q_TILE), lam_init)
        zeros = jnp.zeros((nbp, n_state), F32)
        xp, hr_p, hi_p = mixer_tail(xp, att, u, g_att, g_s5, zeros, zeros, nbp, seq, _tile(seq, S5_TIME_TILE))
        qs, ks, vs, us, gs_att, gs_s5 = _inproj(xs, g_mix, w_in_b, _tile(xs.shape[0], ROW_TILE))
        new_shp = (nbs, n_new * N_HEADS, HEAD_W)
        att_s = _sattn(page_table, qs.reshape(new_shp), ks.reshape(new_shp), vs.reshape(new_shp), ck, cv, lams, sg,
                       lam_init).reshape(nbs * n_new, d)
        xs, hr_s, hi_s = mixer_tail(xs, att_s, us, gs_att, gs_s5, state_s5_re[l].reshape(nbs, n_state),
                                    state_s5_im[l].reshape(nbs, n_state), nbs, n_new, n_new)
        kv_p = (nbp, seq, N_HEADS, HEAD_W)
        kv_s = (nbs, n_new, N_HEADS, HEAD_W)
        st = (S5_GROUPS, S5_STATE)
        for lst, val in zip(outs, (k.reshape(kv_p), v.reshape(kv_p), hr_p.reshape(nbp, *st), hi_p.reshape(nbp, *st),
                                   ks.reshape(kv_s), vs.reshape(kv_s), hr_s.reshape(nbs, *st),
                                   hi_s.reshape(nbs, *st))):
            lst.append(val)
    assert depth == 1
    return (xp.reshape(nbp, seq, d), xs.reshape(nbs, n_new, d)) + tuple(jnp.stack(o) for o in outs)
```
